```python
import jax, jax.numpy as jnp
from jax import lax
import numpy as np

D_MODEL = 1024
BATCH = 4
SEQ = 4096
DEPTH = 4

N_MIXERS = 3
EXPAND = 2
D_INNER = EXPAND * D_MODEL
POOL_WINDOWS = (2, 4, 8, 16)
N_POOL_GROUPS = len(POOL_WINDOWS)
POOL_GROUP = D_INNER // N_POOL_GROUPS
CONV_WIDTH = 3
N_HEADS = 16
QK_NOPE_DIM = 128
QK_ROPE_DIM = 64
V_HEAD_DIM = D_INNER // N_HEADS
Q_LORA_RANK = 384
KV_LORA_RANK = 256
MLA_IN_DIM = Q_LORA_RANK + KV_LORA_RANK + QK_ROPE_DIM + D_INNER
ATTN_SCALE = (QK_NOPE_DIM + QK_ROPE_DIM) ** -0.5
ROPE_BASE = 10000.0
Q_BLOCK = 128
NORM_EPS = 1e-6
MAX_POS_OFFSET = 1024
N_POOL = (DEPTH + 2) // 3
N_CONV = (DEPTH + 1) // 3
N_MLA = DEPTH // 3

kernel_name = "hybrid_pool_conv_mla_gated_trunk"


def rms_norm(x, g):
    xf = x.astype(jnp.float32)
    y = xf * lax.rsqrt(jnp.mean(xf * xf, axis=-1, keepdims=True) + NORM_EPS)
    return (y * g.astype(jnp.float32)).astype(x.dtype)


def pool_mixer(xn, w_in, w_grp, scale, w_out):
    B, S, _ = xn.shape
    u, z = jnp.split(xn @ w_in, 2, axis=-1)
    uf = u.astype(jnp.float32).reshape(B, S, N_POOL_GROUPS, POOL_GROUP)
    cs = jnp.cumsum(uf, axis=1)
    count_base = jnp.arange(1, S + 1, dtype=jnp.float32)
    pooled = []
    for g, w in enumerate(POOL_WINDOWS):
        c = cs[:, :, g]
        prev = jnp.pad(c, ((0, 0), (w, 0), (0, 0)))[:, :S]
        mean = (c - prev) / jnp.minimum(count_base, float(w))[None, :, None]
        pooled.append(mean - uf[:, :, g])
    pooled = jnp.stack(pooled, axis=2).astype(u.dtype)
    mixed = jnp.einsum('bsgc,gcd->bsgd', pooled, w_grp).reshape(B, S, D_INNER) * scale
    return (mixed * jax.nn.silu(z)) @ w_out


def causal_depthwise_conv(x, w):
    return lax.conv_general_dilated(
        x, w[:, None, :].astype(x.dtype), window_strides=(1,),
        padding=[(CONV_WIDTH - 1, 0)], dimension_numbers=('NWC', 'WIO', 'NWC'),
        feature_group_count=x.shape[-1])


def conv_mixer(xn, w_in, conv_w, w_out):
    b, c, h, z = jnp.split(xn @ w_in, 4, axis=-1)
    y = b * causal_depthwise_conv(c * h, conv_w)
    return (y * jax.nn.silu(z)) @ w_out


def apply_rope(x, cos, sin):
    half = x.shape[-1] // 2
    x1, x2 = x[..., :half], x[..., half:]
    return jnp.concatenate([x1 * cos - x2 * sin, x2 * cos + x1 * sin], axis=-1).astype(x.dtype)


def causal_block_attention(q_nope, q_rope, k_nope, k_rope, v):
    B, S, H, _ = q_nope.shape
    nb = S // Q_BLOCK
    qn = q_nope.reshape(B, nb, Q_BLOCK, H, QK_NOPE_DIM).transpose(1, 0, 2, 3, 4)
    qr = q_rope.reshape(B, nb, Q_BLOCK, H, QK_ROPE_DIM).transpose(1, 0, 2, 3, 4)
    starts = jnp.arange(nb, dtype=jnp.int32) * Q_BLOCK
    key_idx = jnp.arange(S, dtype=jnp.int32)

    def one_block(args):
        qn_b, qr_b, start = args
        s = (jnp.einsum('bqhd,bkhd->bhqk', qn_b, k_nope).astype(jnp.float32)
             + jnp.einsum('bqhr,bkr->bhqk', qr_b, k_rope).astype(jnp.float32)) * ATTN_SCALE
        q_idx = start + jnp.arange(Q_BLOCK, dtype=jnp.int32)
        mask = key_idx[None, :] <= q_idx[:, None]
        s = jnp.where(mask[None, None], s, jnp.float32(-1e30))
        p = jax.nn.softmax(s, axis=-1).astype(v.dtype)
        return jnp.einsum('bhqk,bkhd->bqhd', p, v)

    o = lax.map(one_block, (qn, qr, starts))
    return o.transpose(1, 0, 2, 3, 4).reshape(B, S, H, V_HEAD_DIM)


def mla_mixer(xn, cos, sin, w_in, q_norm, w_q_up, kv_norm, w_kv_up, w_out):
    B, S, _ = xn.shape
    h = xn @ w_in
    q_lat, kv_lat, k_rope, z = jnp.split(
        h, [Q_LORA_RANK, Q_LORA_RANK + KV_LORA_RANK,
            Q_LORA_RANK + KV_LORA_RANK + QK_ROPE_DIM], axis=-1)
    q = (rms_norm(q_lat, q_norm) @ w_q_up).reshape(B, S, N_HEADS, QK_NOPE_DIM + QK_ROPE_DIM)
    q_nope = q[..., :QK_NOPE_DIM]
    q_rope = apply_rope(q[..., QK_NOPE_DIM:], cos[:, :, None, :], sin[:, :, None, :])
    kv = (rms_norm(kv_lat, kv_norm) @ w_kv_up).reshape(B, S, N_HEADS, QK_NOPE_DIM + V_HEAD_DIM)
    k_nope, v = kv[..., :QK_NOPE_DIM], kv[..., QK_NOPE_DIM:]
    k_rope = apply_rope(k_rope, cos, sin)
    o = causal_block_attention(q_nope, q_rope, k_nope, k_rope, v)
    return (o.reshape(B, S, D_INNER) * jax.nn.silu(z)) @ w_out


def setup_inputs(seed: int = 0) -> dict:
    key = jax.random.key(seed)
    ks = jax.random.split(key, 20)

    def normal(k, shape, scale):
        return jax.random.normal(k, shape, jnp.float32) * scale

    def gain(k, shape):
        return 1.0 + 0.02 * jax.random.normal(k, shape, jnp.float32)

    x = normal(ks[0], (BATCH, SEQ, D_MODEL), 1.0)
    offset = jax.random.randint(ks[1], (BATCH, 1), 0, MAX_POS_OFFSET, dtype=jnp.int32)
    positions = (offset + jnp.arange(SEQ, dtype=jnp.int32)[None, :]).astype(jnp.int32)
    return {
        "x": x,
        "positions": positions,
        "pool_norm": gain(ks[2], (N_POOL, D_MODEL)),
        "pool_w_in": normal(ks[3], (N_POOL, D_MODEL, 2 * D_INNER), D_MODEL ** -0.5),
        "pool_w_grp": normal(ks[4], (N_POOL, N_POOL_GROUPS, POOL_GROUP, POOL_GROUP), POOL_GROUP ** -0.5),
        "pool_scale": gain(ks[5], (N_POOL, D_INNER)),
        "pool_w_out": normal(ks[6], (N_POOL, D_INNER, D_MODEL), D_INNER ** -0.5),
        "conv_norm": gain(ks[7], (N_CONV, D_MODEL)),
        "conv_w_in": normal(ks[8], (N_CONV, D_MODEL, 4 * D_INNER), D_MODEL ** -0.5),
        "conv_w": normal(ks[9], (N_CONV, CONV_WIDTH, D_INNER), CONV_WIDTH ** -0.5),
        "conv_w_out": normal(ks[10], (N_CONV, D_INNER, D_MODEL), D_INNER ** -0.5),
        "mla_norm": gain(ks[11], (N_MLA, D_MODEL)),
        "mla_w_in": normal(ks[12], (N_MLA, D_MODEL, MLA_IN_DIM), D_MODEL ** -0.5),
        "mla_q_norm": gain(ks[13], (N_MLA, Q_LORA_RANK)),
        "mla_w_q_up": normal(ks[14], (N_MLA, Q_LORA_RANK, N_HEADS * (QK_NOPE_DIM + QK_ROPE_DIM)), Q_LORA_RANK ** -0.5),
        "mla_kv_norm": gain(ks[15], (N_MLA, KV_LORA_RANK)),
        "mla_w_kv_up": normal(ks[16], (N_MLA, KV_LORA_RANK, N_HEADS * (QK_NOPE_DIM + V_HEAD_DIM)), KV_LORA_RANK ** -0.5),
        "mla_w_out": normal(ks[17], (N_MLA, D_INNER, D_MODEL), D_INNER ** -0.5),
        "final_norm": gain(ks[18], (D_MODEL,)),
    }


def reference(x, positions, pool_norm, pool_w_in, pool_w_grp, pool_scale, pool_w_out,
              conv_norm, conv_w_in, conv_w, conv_w_out,
              mla_norm, mla_w_in, mla_q_norm, mla_w_q_up, mla_kv_norm, mla_w_kv_up, mla_w_out,
              final_norm):
    inv_freq = ROPE_BASE ** (-jnp.arange(0, QK_ROPE_DIM, 2, dtype=jnp.float32) / QK_ROPE_DIM)
    angles = positions.astype(jnp.float32)[..., None] * inv_freq
    cos, sin = jnp.cos(angles).astype(x.dtype), jnp.sin(angles).astype(x.dtype)

    for i in range(DEPTH):
        kind, j = i % N_MIXERS, i // N_MIXERS
        if kind == 0:
            xn = rms_norm(x, pool_norm[j])
            x = x + pool_mixer(xn, pool_w_in[j], pool_w_grp[j], pool_scale[j], pool_w_out[j])
        elif kind == 1:
            xn = rms_norm(x, conv_norm[j])
            x = x + conv_mixer(xn, conv_w_in[j], conv_w[j], conv_w_out[j])
        else:
            xn = rms_norm(x, mla_norm[j])
            x = x + mla_mixer(xn, cos, sin, mla_w_in[j], mla_q_norm[j], mla_w_q_up[j],
                              mla_kv_norm[j], mla_w_kv_up[j], mla_w_out[j])
    return rms_norm(x, final_norm)
```

```python
import functools

import numpy as np
import jax
import jax.numpy as jnp
from jax import lax
from jax.experimental import pallas as pl
from jax.experimental.pallas import tpu as pltpu

D_MODEL = 1024
DEPTH = 4
N_MIXERS = 3
D_INNER = 2 * D_MODEL
POOL_WINDOWS = (2, 4, 8, 16)
N_POOL_GROUPS = len(POOL_WINDOWS)
POOL_GROUP = D_INNER // N_POOL_GROUPS
CONV_WIDTH = 3
N_HEADS = 16
N_PAIRS = N_HEADS // 2
QK_NOPE_DIM = 128
QK_ROPE_DIM = 64
ROPE_HALF = QK_ROPE_DIM // 2
V_HEAD_DIM = D_INNER // N_HEADS
Q_LORA_RANK = 384
KV_LORA_RANK = 256
Q_HEAD_DIM = QK_NOPE_DIM + QK_ROPE_DIM
ATTN_SCALE = Q_HEAD_DIM ** -0.5
ROPE_BASE = 10000.0
NORM_EPS = 1e-6
MASK_VALUE = -1e30

V7X_LANES = 128
V7X_SUBLANES = 8
V7X_VMEM_LIMIT_BYTES = 56 * 1024 * 1024

POOL_HALO = max(POOL_WINDOWS)
CONV_HALO = V7X_SUBLANES
SEQ_TILE = 512
CONV_CHUNK = 512
ATTN_TQ = 256
ATTN_TK = 256
PAIR_Q = 2 * QK_NOPE_DIM + 2 * QK_ROPE_DIM
HEAD_K = 2 * QK_NOPE_DIM
MLA_H = Q_LORA_RANK + KV_LORA_RANK + V7X_LANES + D_INNER

F32 = jnp.float32
BF16 = jnp.bfloat16


def _rms_scale(x, g):
    ms = jnp.mean(x * x, axis=-1, keepdims=True)
    return x * lax.rsqrt(ms + NORM_EPS) * g


def _silu(z):
    return z * (1.0 / (1.0 + jnp.exp(-z)))


def _dot(a, b):
    return jnp.dot(a, b, preferred_element_type=F32)


def _const_spec(shape):
    zeros = (0,) * len(shape)
    return pl.BlockSpec(shape, lambda *_: zeros, pipeline_mode=pl.Buffered(1))


def _row_spec(tile, width):
    return pl.BlockSpec((None, tile, width), lambda b, s: (b, s, 0))


def _params(semantics):
    return pltpu.CompilerParams(dimension_semantics=semantics,
                                vmem_limit_bytes=V7X_VMEM_LIMIT_BYTES)


def _pool_kernel(*refs, tile, final):
    if final:
        x_ref, g_ref, win_ref, wgrp_ref, scale_ref, wout_ref, fin_ref, o_ref, tail_ref = refs
    else:
        x_ref, g_ref, win_ref, wgrp_ref, scale_ref, wout_ref, o_ref, tail_ref = refs
    s_idx = pl.program_id(1)

    @pl.when(s_idx == 0)
    def _():
        tail_ref[...] = jnp.zeros_like(tail_ref)

    x = x_ref[...]
    xn = _rms_scale(x, g_ref[...]).astype(BF16)
    pos = s_idx * tile + lax.broadcasted_iota(jnp.int32, (tile, 1), 0)
    count = (pos + 1).astype(F32)
    acc = None
    for g, w in enumerate(POOL_WINDOWS):
        lo, hi = g * POOL_GROUP, (g + 1) * POOL_GROUP
        u = _dot(xn, win_ref[:, lo:hi])
        z = _dot(xn, win_ref[:, D_INNER + lo:D_INNER + hi])
        s = jnp.concatenate([tail_ref[:, lo:hi], u], axis=0)
        k = 1
        while k < w:
            s = s + pltpu.roll(s, k, axis=0)
            k *= 2
        s = s[POOL_HALO:, :]
        inv = 1.0 / jnp.minimum(count, float(w))
        pooled = (s * inv - u).astype(BF16)
        mixed = _dot(pooled, wgrp_ref[g]) * scale_ref[:, lo:hi]
        gated = (mixed * _silu(z)).astype(BF16)
        part = _dot(gated, wout_ref[lo:hi, :])
        acc = part if acc is None else acc + part
        tail_ref[:, lo:hi] = u[tile - POOL_HALO:, :]
    y = x + acc
    if final:
        y = _rms_scale(y, fin_ref[...])
    o_ref[...] = y


def _pool_layer(x, g, w_in, w_grp, scale, w_out, final_g=None):
    B, S, D = x.shape
    tile = SEQ_TILE
    final = final_g is not None
    in_specs = [
        _row_spec(tile, D),
        _const_spec((1, D)),
        _const_spec((D, 2 * D_INNER)),
        _const_spec((N_POOL_GROUPS, POOL_GROUP, POOL_GROUP)),
        _const_spec((1, D_INNER)),
        _const_spec((D_INNER, D)),
    ]
    args = [x, g.reshape(1, D), w_in, w_grp, scale.reshape(1, D_INNER), w_out]
    if final:
        in_specs.append(_const_spec((1, D)))
        args.append(final_g.reshape(1, D))
    return pl.pallas_call(
        functools.partial(_pool_kernel, tile=tile, final=final),
        grid=(B, S // tile),
        in_specs=in_specs,
        out_specs=_row_spec(tile, D),
        out_shape=jax.ShapeDtypeStruct((B, S, D), F32),
        scratch_shapes=[pltpu.VMEM((POOL_HALO, D_INNER), F32)],
        compiler_params=_params(("arbitrary", "arbitrary")),
        name="pool_layer",
    )(*args)


def _conv_kernel(x_ref, g_ref, win_ref, cw_ref, wout_ref, o_ref, tail_ref, *, tile):
    s_idx = pl.program_id(1)

    @pl.when(s_idx == 0)
    def _():
        tail_ref[...] = jnp.zeros_like(tail_ref)

    x = x_ref[...]
    xn = _rms_scale(x, g_ref[...]).astype(BF16)
    acc = None
    for c in range(D_INNER // CONV_CHUNK):
        lo, hi = c * CONV_CHUNK, (c + 1) * CONV_CHUNK
        bb = _dot(xn, win_ref[:, lo:hi])
        cc = _dot(xn, win_ref[:, D_INNER + lo:D_INNER + hi])
        hh = _dot(xn, win_ref[:, 2 * D_INNER + lo:2 * D_INNER + hi])
        zz = _dot(xn, win_ref[:, 3 * D_INNER + lo:3 * D_INNER + hi])
        ch = cc * hh
        che = jnp.concatenate([tail_ref[:, lo:hi], ch], axis=0)
        cw = cw_ref[:, lo:hi]
        conv = (che * cw[2:3, :] + pltpu.roll(che, 1, axis=0) * cw[1:2, :]
                + pltpu.roll(che, 2, axis=0) * cw[0:1, :])[CONV_HALO:, :]
        y = (bb * conv * _silu(zz)).astype(BF16)
        part = _dot(y, wout_ref[lo:hi, :])
        acc = part if acc is None else acc + part
        tail_ref[:, lo:hi] = ch[tile - CONV_HALO:, :]
    o_ref[...] = x + acc


def _conv_layer(x, g, w_in, conv_w, w_out):
    B, S, D = x.shape
    tile = SEQ_TILE
    return pl.pallas_call(
        functools.partial(_conv_kernel, tile=tile),
        grid=(B, S // tile),
        in_specs=[
            _row_spec(tile, D),
            _const_spec((1, D)),
            _const_spec((D, 4 * D_INNER)),
            _const_spec((CONV_WIDTH, D_INNER)),
            _const_spec((D_INNER, D)),
        ],
        out_specs=_row_spec(tile, D),
        out_shape=jax.ShapeDtypeStruct((B, S, D), F32),
        scratch_shapes=[pltpu.VMEM((CONV_HALO, D_INNER), F32)],
        compiler_params=_params(("arbitrary", "arbitrary")),
        name="conv_layer",
    )(x, g.reshape(1, D), w_in, conv_w, w_out)


def _rope_constants():
    inv_freq = ROPE_BASE ** (-np.arange(0, QK_ROPE_DIM, 2, dtype=np.float32) / QK_ROPE_DIM)
    lane_group = np.arange(V7X_LANES) // ROPE_HALF
    rows = np.zeros((V7X_SUBLANES, V7X_LANES), np.float32)
    rows[0] = np.tile(inv_freq.astype(np.float32), V7X_LANES // ROPE_HALF)
    rows[1] = np.where(lane_group < 2, -1.0, 1.0)
    rows[2] = (lane_group % 2 == 0)
    rows[3] = (lane_group % 2 == 1)
    return rows


def _mla_column_orders():
    half = np.arange(ROPE_HALF)
    kr0 = Q_LORA_RANK + KV_LORA_RANK
    w_in_cols = np.concatenate([
        np.arange(kr0),
        kr0 + half, kr0 + half, kr0 + ROPE_HALF + half, kr0 + ROPE_HALF + half,
        np.arange(kr0 + QK_ROPE_DIM, kr0 + QK_ROPE_DIM + D_INNER)])
    q_cols = []
    for p in range(N_PAIRS):
        h0, h1 = 2 * p * Q_HEAD_DIM, (2 * p + 1) * Q_HEAD_DIM
        nope = np.arange(QK_NOPE_DIM)
        q_cols += [h0 + nope,
                   h0 + QK_NOPE_DIM + half, h1 + QK_NOPE_DIM + half,
                   h0 + QK_NOPE_DIM + ROPE_HALF + half, h1 + QK_NOPE_DIM + ROPE_HALF + half,
                   h1 + nope]
    return w_in_cols, np.concatenate(q_cols)


def _mla_proj_kernel(x_ref, pos_ref, g_ref, win_ref, qn_ref, wq_ref, kvn_ref, wkv_ref, rc_ref,
                     q_ref, k_ref, v_ref, z_ref):
    xn = _rms_scale(x_ref[...], g_ref[...]).astype(BF16)
    kv0 = Q_LORA_RANK
    kr0 = kv0 + KV_LORA_RANK
    z0 = kr0 + V7X_LANES
    q_lat = _dot(xn, win_ref[:, :kv0])
    kv_lat = _dot(xn, win_ref[:, kv0:kr0])
    kr = _dot(xn, win_ref[:, kr0:z0])
    z_ref[...] = _dot(xn, win_ref[:, z0:])

    angles = pos_ref[...].astype(F32) * rc_ref[0:1, :]
    cos = jnp.cos(angles)
    sin = jnp.sin(angles) * rc_ref[1:2, :]

    def rope(t):
        return t * cos + pltpu.roll(t, 2 * ROPE_HALF, axis=1) * sin

    qn = _rms_scale(q_lat, qn_ref[...]).astype(BF16)
    for p in range(N_PAIRS):
        lo = p * PAIR_Q
        qp = _dot(qn, wq_ref[:, lo:lo + PAIR_Q])
        q_ref[:, lo:lo + QK_NOPE_DIM] = qp[:, :QK_NOPE_DIM].astype(BF16)
        q_ref[:, lo + QK_NOPE_DIM:lo + 2 * QK_NOPE_DIM] = (
            rope(qp[:, QK_NOPE_DIM:2 * QK_NOPE_DIM]).astype(BF16))
        q_ref[:, lo + 2 * QK_NOPE_DIM:lo + PAIR_Q] = qp[:, 2 * QK_NOPE_DIM:].astype(BF16)

    kr = rope(kr)
    kr_even = (kr * rc_ref[2:3, :]).astype(BF16)
    kr_odd = (kr * rc_ref[3:4, :]).astype(BF16)
    kvn = _rms_scale(kv_lat, kvn_ref[...]).astype(BF16)
    for h in range(N_HEADS):
        lo = h * HEAD_K
        kv = _dot(kvn, wkv_ref[:, lo:lo + HEAD_K])
        k_nope = kv[:, :QK_NOPE_DIM].astype(BF16)
        v_ref[:, h * V_HEAD_DIM:(h + 1) * V_HEAD_DIM] = kv[:, QK_NOPE_DIM:].astype(BF16)
        if h % 2 == 0:
            k_ref[:, lo:lo + QK_NOPE_DIM] = k_nope
            k_ref[:, lo + QK_NOPE_DIM:lo + HEAD_K] = kr_even
        else:
            k_ref[:, lo:lo + QK_NOPE_DIM] = kr_odd
            k_ref[:, lo + QK_NOPE_DIM:lo + HEAD_K] = k_nope


def _mla_proj(x, positions, g, w_in, q_norm, w_q_up, kv_norm, w_kv_up):
    B, S, D = x.shape
    tile = SEQ_TILE
    n_q, n_k = N_PAIRS * PAIR_Q, N_HEADS * HEAD_K
    return pl.pallas_call(
        _mla_proj_kernel,
        grid=(B, S // tile),
        in_specs=[
            _row_spec(tile, D),
            _row_spec(tile, 1),
            _const_spec((1, D)),
            _const_spec((D, MLA_H)),
            _const_spec((1, Q_LORA_RANK)),
            _const_spec((Q_LORA_RANK, n_q)),
            _const_spec((1, KV_LORA_RANK)),
            _const_spec((KV_LORA_RANK, n_k)),
            _const_spec((V7X_SUBLANES, V7X_LANES)),
        ],
        out_specs=[_row_spec(tile, n_q), _row_spec(tile, n_k),
                   _row_spec(tile, D_INNER), _row_spec(tile, D_INNER)],
        out_shape=[jax.ShapeDtypeStruct((B, S, n_q), BF16),
                   jax.ShapeDtypeStruct((B, S, n_k), BF16),
                   jax.ShapeDtypeStruct((B, S, D_INNER), BF16),
                   jax.ShapeDtypeStruct((B, S, D_INNER), F32)],
        compiler_params=_params(("arbitrary", "arbitrary")),
        name="mla_proj",
    )(x, positions.reshape(B, S, 1), g.reshape(1, D), w_in, q_norm.reshape(1, -1), w_q_up,
      kv_norm.reshape(1, -1), w_kv_up, jnp.asarray(_rope_constants()))


def _attn_kernel(q_ref, k_ref, v_ref, z_ref, o_ref, *, seq):
    nt = (((1,), (1,)), ((), ()))
    for hh in range(2):
        q_lo = hh * QK_NOPE_DIM
        k_lo = hh * HEAD_K
        v_lo = hh * V_HEAD_DIM

        def q_body(qi, _, q_lo=q_lo, k_lo=k_lo, v_lo=v_lo):
            q0 = pl.multiple_of(qi * ATTN_TQ, ATTN_TQ)
            q = q_ref[pl.ds(q0, ATTN_TQ), q_lo:q_lo + HEAD_K]

            def step(kj, carry, masked):
                m, l, acc = carry
                k0 = pl.multiple_of(kj * ATTN_TK, ATTN_TK)
                k = k_ref[pl.ds(k0, ATTN_TK), k_lo:k_lo + HEAD_K]
                v = v_ref[pl.ds(k0, ATTN_TK), v_lo:v_lo + V_HEAD_DIM]
                s = lax.dot_general(q, k, nt, preferred_element_type=F32) * ATTN_SCALE
                if masked:
                    q_idx = q0 + lax.broadcasted_iota(jnp.int32, (ATTN_TQ, ATTN_TK), 0)
                    k_idx = k0 + lax.broadcasted_iota(jnp.int32, (ATTN_TQ, ATTN_TK), 1)
                    s = jnp.where(k_idx <= q_idx, s, MASK_VALUE)
                m_new = jnp.maximum(m, jnp.max(s, axis=-1, keepdims=True))
                alpha = jnp.exp(m - m_new)
                p = jnp.exp(s - m_new)
                l = alpha * l + jnp.sum(p, axis=-1, keepdims=True)
                acc = alpha * acc + _dot(p.astype(BF16), v)
                return m_new, l, acc

            init = (jnp.full((ATTN_TQ, 1), MASK_VALUE, F32),
                    jnp.zeros((ATTN_TQ, 1), F32),
                    jnp.zeros((ATTN_TQ, V_HEAD_DIM), F32))
            carry = lax.fori_loop(0, qi, functools.partial(step, masked=False), init)
            _, l, acc = step(qi, carry, masked=True)
            zz = z_ref[pl.ds(q0, ATTN_TQ), v_lo:v_lo + V_HEAD_DIM]
            o_ref[pl.ds(q0, ATTN_TQ), v_lo:v_lo + V_HEAD_DIM] = (
                (acc / l) * _silu(zz)).astype(BF16)
            return 0

        lax.fori_loop(0, seq // ATTN_TQ, q_body, 0)


def _attention(q, k, v, z):
    B, S, _ = q.shape
    pair_v = 2 * V_HEAD_DIM

    def spec(width):
        return pl.BlockSpec((None, S, width), lambda b, p: (b, 0, p))

    return pl.pallas_call(
        functools.partial(_attn_kernel, seq=S),
        grid=(B, N_PAIRS),
        in_specs=[spec(PAIR_Q), spec(2 * HEAD_K), spec(pair_v), spec(pair_v)],
        out_specs=spec(pair_v),
        out_shape=jax.ShapeDtypeStruct((B, S, D_INNER), BF16),
        compiler_params=_params(("arbitrary", "arbitrary")),
        name="mla_attention",
    )(q, k, v, z)


def _out_proj_kernel(x_ref, a_ref, w_ref, o_ref):
    o_ref[...] = x_ref[...] + _dot(a_ref[...], w_ref[...])


def _out_proj(x, a, w_out):
    B, S, D = x.shape
    tile = SEQ_TILE
    return pl.pallas_call(
        _out_proj_kernel,
        grid=(B, S // tile),
        in_specs=[_row_spec(tile, D), _row_spec(tile, D_INNER), _const_spec((D_INNER, D))],
        out_specs=_row_spec(tile, D),
        out_shape=jax.ShapeDtypeStruct((B, S, D), F32),
        compiler_params=_params(("arbitrary", "arbitrary")),
        name="mla_out_proj",
    )(x, a, w_out)


def _mla_layer(x, positions, g, w_in, q_norm, w_q_up, kv_norm, w_kv_up, w_out):
    w_in_cols, q_cols = _mla_column_orders()
    q, k, v, z = _mla_proj(x, positions, g, w_in[:, w_in_cols].astype(BF16), q_norm,
                           w_q_up[:, q_cols].astype(BF16), kv_norm, w_kv_up.astype(BF16))
    gated = _attention(q, k, v, z)
    return _out_proj(x, gated, w_out.astype(BF16))


def kernel(x, positions, pool_norm, pool_w_in, pool_w_grp, pool_scale, pool_w_out,
           conv_norm, conv_w_in, conv_w, conv_w_out,
           mla_norm, mla_w_in, mla_q_norm, mla_w_q_up, mla_kv_norm, mla_w_kv_up, mla_w_out,
           final_norm):
    for i in range(DEPTH):
        kind, j = i % N_MIXERS, i // N_MIXERS
        last = i == DEPTH - 1
        if kind == 0:
            x = _pool_layer(x, pool_norm[j], pool_w_in[j].astype(BF16),
                            pool_w_grp[j].astype(BF16), pool_scale[j],
                            pool_w_out[j].astype(BF16), final_norm if last else None)
        elif kind == 1:
            x = _conv_layer(x, conv_norm[j], conv_w_in[j].astype(BF16), conv_w[j],
                            conv_w_out[j].astype(BF16))
        else:
            x = _mla_layer(x, positions, mla_norm[j], mla_w_in[j], mla_q_norm[j],
                           mla_w_q_up[j], mla_kv_norm[j], mla_w_kv_up[j], mla_w_out[j])
        if last and kind != 0:
            raise NotImplementedError("final norm is fused into a pool layer")
    return x
```

```python
import functools
import math

import numpy as np
import jax
import jax.numpy as jnp
from jax import lax
from jax.experimental import pallas as pl
from jax.experimental.pallas import tpu as pltpu

D_MODEL = 1024
DEPTH = 4
N_MIXERS = 3
D_INNER = 2 * D_MODEL
POOL_WINDOWS = (2, 4, 8, 16)
N_POOL_GROUPS = len(POOL_WINDOWS)
POOL_GROUP = D_INNER // N_POOL_GROUPS
CONV_WIDTH = 3
N_HEADS = 16
N_PAIRS = N_HEADS // 2
QK_NOPE_DIM = 128
QK_ROPE_DIM = 64
ROPE_HALF = QK_ROPE_DIM // 2
V_HEAD_DIM = D_INNER // N_HEADS
Q_LORA_RANK = 384
KV_LORA_RANK = 256
Q_HEAD_DIM = QK_NOPE_DIM + QK_ROPE_DIM
ATTN_SCALE = Q_HEAD_DIM ** -0.5
ROPE_BASE = 10000.0
NORM_EPS = 1e-6
MASK_VALUE = -1e30

V7X_LANES = 128
V7X_SUBLANES = 8
V7X_VMEM_LIMIT_BYTES = 56 * 1024 * 1024

POOL_HALO = max(POOL_WINDOWS)
CONV_HALO = V7X_SUBLANES
SEQ_TILE = 512
CONV_CHUNK = 512
ATTN_TQ = 512
ATTN_TK = 512
ATTN_UNROLL = 2
EXP2_SCALE = ATTN_SCALE * math.log2(math.e)
PAIR_Q = 2 * QK_NOPE_DIM + 2 * QK_ROPE_DIM
HEAD_K = 2 * QK_NOPE_DIM
MLA_H = Q_LORA_RANK + KV_LORA_RANK + V7X_LANES + D_INNER

F32 = jnp.float32
BF16 = jnp.bfloat16


def _rms_scale(x, g):
    ms = jnp.mean(x * x, axis=-1, keepdims=True)
    return x * lax.rsqrt(ms + NORM_EPS) * g


def _silu(z):
    return z * (1.0 / (1.0 + jnp.exp(-z)))


def _dot(a, b):
    return jnp.dot(a, b, preferred_element_type=F32)


def _const_spec(shape):
    zeros = (0,) * len(shape)
    return pl.BlockSpec(shape, lambda *_: zeros, pipeline_mode=pl.Buffered(1))


def _row_spec(tile, width):
    return pl.BlockSpec((None, tile, width), lambda b, s: (b, s, 0))


def _params(semantics):
    return pltpu.CompilerParams(dimension_semantics=semantics,
                                vmem_limit_bytes=V7X_VMEM_LIMIT_BYTES)


def _pool_kernel(*refs, tile, final):
    if final:
        x_ref, g_ref, win_ref, wgrp_ref, scale_ref, wout_ref, fin_ref, o_ref, tail_ref = refs
    else:
        x_ref, g_ref, win_ref, wgrp_ref, scale_ref, wout_ref, o_ref, tail_ref = refs
    s_idx = pl.program_id(1)

    @pl.when(s_idx == 0)
    def _():
        tail_ref[...] = jnp.zeros_like(tail_ref)

    x = x_ref[...]
    xn = _rms_scale(x, g_ref[...]).astype(BF16)
    pos = s_idx * tile + lax.broadcasted_iota(jnp.int32, (tile, 1), 0)
    count = (pos + 1).astype(F32)
    acc = None
    for g, w in enumerate(POOL_WINDOWS):
        lo, hi = g * POOL_GROUP, (g + 1) * POOL_GROUP
        u = _dot(xn, win_ref[:, lo:hi])
        z = _dot(xn, win_ref[:, D_INNER + lo:D_INNER + hi])
        s = jnp.concatenate([tail_ref[:, lo:hi], u], axis=0)
        k = 1
        while k < w:
            s = s + pltpu.roll(s, k, axis=0)
            k *= 2
        s = s[POOL_HALO:, :]
        inv = 1.0 / jnp.minimum(count, float(w))
        pooled = (s * inv - u).astype(BF16)
        mixed = _dot(pooled, wgrp_ref[g]) * scale_ref[:, lo:hi]
        gated = (mixed * _silu(z)).astype(BF16)
        part = _dot(gated, wout_ref[lo:hi, :])
        acc = part if acc is None else acc + part
        tail_ref[:, lo:hi] = u[tile - POOL_HALO:, :]
    y = x + acc
    if final:
        y = _rms_scale(y, fin_ref[...])
    o_ref[...] = y


def _pool_layer(x, g, w_in, w_grp, scale, w_out, final_g=None):
    B, S, D = x.shape
    tile = SEQ_TILE
    final = final_g is not None
    in_specs = [
        _row_spec(tile, D),
        _const_spec((1, D)),
        _const_spec((D, 2 * D_INNER)),
        _const_spec((N_POOL_GROUPS, POOL_GROUP, POOL_GROUP)),
        _const_spec((1, D_INNER)),
        _const_spec((D_INNER, D)),
    ]
    args = [x, g.reshape(1, D), w_in, w_grp, scale.reshape(1, D_INNER), w_out]
    if final:
        in_specs.append(_const_spec((1, D)))
        args.append(final_g.reshape(1, D))
    return pl.pallas_call(
        functools.partial(_pool_kernel, tile=tile, final=final),
        grid=(B, S // tile),
        in_specs=in_specs,
        out_specs=_row_spec(tile, D),
        out_shape=jax.ShapeDtypeStruct((B, S, D), F32),
        scratch_shapes=[pltpu.VMEM((POOL_HALO, D_INNER), F32)],
        compiler_params=_params(("arbitrary", "arbitrary")),
        name="pool_layer",
    )(*args)


def _conv_kernel(x_ref, g_ref, win_ref, cw_ref, wout_ref, o_ref, tail_ref, *, tile):
    s_idx = pl.program_id(1)

    @pl.when(s_idx == 0)
    def _():
        tail_ref[...] = jnp.zeros_like(tail_ref)

    x = x_ref[...]
    xn = _rms_scale(x, g_ref[...]).astype(BF16)
    acc = None
    for c in range(D_INNER // CONV_CHUNK):
        lo, hi = c * CONV_CHUNK, (c + 1) * CONV_CHUNK
        bb = _dot(xn, win_ref[:, lo:hi])
        cc = _dot(xn, win_ref[:, D_INNER + lo:D_INNER + hi])
        hh = _dot(xn, win_ref[:, 2 * D_INNER + lo:2 * D_INNER + hi])
        zz = _dot(xn, win_ref[:, 3 * D_INNER + lo:3 * D_INNER + hi])
        ch = cc * hh
        che = jnp.concatenate([tail_ref[:, lo:hi], ch], axis=0)
        cw = cw_ref[:, lo:hi]
        conv = (che * cw[2:3, :] + pltpu.roll(che, 1, axis=0) * cw[1:2, :]
                + pltpu.roll(che, 2, axis=0) * cw[0:1, :])[CONV_HALO:, :]
        y = (bb * conv * _silu(zz)).astype(BF16)
        part = _dot(y, wout_ref[lo:hi, :])
        acc = part if acc is None else acc + part
        tail_ref[:, lo:hi] = ch[tile - CONV_HALO:, :]
    o_ref[...] = x + acc


def _conv_layer(x, g, w_in, conv_w, w_out):
    B, S, D = x.shape
    tile = SEQ_TILE
    return pl.pallas_call(
        functools.partial(_conv_kernel, tile=tile),
        grid=(B, S // tile),
        in_specs=[
            _row_spec(tile, D),
            _const_spec((1, D)),
            _const_spec((D, 4 * D_INNER)),
            _const_spec((CONV_WIDTH, D_INNER)),
            _const_spec((D_INNER, D)),
        ],
        out_specs=_row_spec(tile, D),
        out_shape=jax.ShapeDtypeStruct((B, S, D), F32),
        scratch_shapes=[pltpu.VMEM((CONV_HALO, D_INNER), F32)],
        compiler_params=_params(("arbitrary", "arbitrary")),
        name="conv_layer",
    )(x, g.reshape(1, D), w_in, conv_w, w_out)


def _rope_constants():
    inv_freq = ROPE_BASE ** (-np.arange(0, QK_ROPE_DIM, 2, dtype=np.float32) / QK_ROPE_DIM)
    lane_group = np.arange(V7X_LANES) // ROPE_HALF
    rows = np.zeros((V7X_SUBLANES, V7X_LANES), np.float32)
    rows[0] = np.tile(inv_freq.astype(np.float32), V7X_LANES // ROPE_HALF)
    rows[1] = np.where(lane_group < 2, -1.0, 1.0)
    rows[2] = (lane_group % 2 == 0)
    rows[3] = (lane_group % 2 == 1)
    return rows


def _mla_column_orders():
    half = np.arange(ROPE_HALF)
    kr0 = Q_LORA_RANK + KV_LORA_RANK
    w_in_cols = np.concatenate([
        np.arange(kr0),
        kr0 + half, kr0 + half, kr0 + ROPE_HALF + half, kr0 + ROPE_HALF + half,
        np.arange(kr0 + QK_ROPE_DIM, kr0 + QK_ROPE_DIM + D_INNER)])
    q_cols = []
    for p in range(N_PAIRS):
        h0, h1 = 2 * p * Q_HEAD_DIM, (2 * p + 1) * Q_HEAD_DIM
        nope = np.arange(QK_NOPE_DIM)
        q_cols += [h0 + nope,
                   h0 + QK_NOPE_DIM + half, h1 + QK_NOPE_DIM + half,
                   h0 + QK_NOPE_DIM + ROPE_HALF + half, h1 + QK_NOPE_DIM + ROPE_HALF + half,
                   h1 + nope]
    return w_in_cols, np.concatenate(q_cols)


def _mla_proj_kernel(x_ref, pos_ref, g_ref, win_ref, qn_ref, wq_ref, kvn_ref, wkv_ref, rc_ref,
                     q_ref, k_ref, v_ref, z_ref):
    xn = _rms_scale(x_ref[...], g_ref[...]).astype(BF16)
    kv0 = Q_LORA_RANK
    kr0 = kv0 + KV_LORA_RANK
    z0 = kr0 + V7X_LANES
    q_lat = _dot(xn, win_ref[:, :kv0])
    kv_lat = _dot(xn, win_ref[:, kv0:kr0])
    kr = _dot(xn, win_ref[:, kr0:z0])
    z_ref[...] = _dot(xn, win_ref[:, z0:])

    angles = pos_ref[...].astype(F32) * rc_ref[0:1, :]
    cos = jnp.cos(angles)
    sin = jnp.sin(angles) * rc_ref[1:2, :]

    def rope(t):
        return t * cos + pltpu.roll(t, 2 * ROPE_HALF, axis=1) * sin

    qn = _rms_scale(q_lat, qn_ref[...]).astype(BF16)
    for p in range(N_PAIRS):
        qp = _dot(qn, wq_ref[:, p * PAIR_Q:(p + 1) * PAIR_Q])
        q_rope = rope(qp[:, QK_NOPE_DIM:2 * QK_NOPE_DIM]).astype(BF16)
        lo = 2 * p * HEAD_K
        q_ref[:, lo:lo + QK_NOPE_DIM] = qp[:, :QK_NOPE_DIM].astype(BF16)
        q_ref[:, lo + QK_NOPE_DIM:lo + HEAD_K] = q_rope
        q_ref[:, lo + HEAD_K:lo + HEAD_K + QK_NOPE_DIM] = q_rope
        q_ref[:, lo + HEAD_K + QK_NOPE_DIM:lo + 2 * HEAD_K] = qp[:, 2 * QK_NOPE_DIM:].astype(BF16)

    kr = rope(kr)
    kr_even = (kr * rc_ref[2:3, :]).astype(BF16)
    kr_odd = (kr * rc_ref[3:4, :]).astype(BF16)
    kvn = _rms_scale(kv_lat, kvn_ref[...]).astype(BF16)
    for h in range(N_HEADS):
        lo = h * HEAD_K
        kv = _dot(kvn, wkv_ref[:, lo:lo + HEAD_K])
        k_nope = kv[:, :QK_NOPE_DIM].astype(BF16)
        v_ref[:, h * V_HEAD_DIM:(h + 1) * V_HEAD_DIM] = kv[:, QK_NOPE_DIM:].astype(BF16)
        if h % 2 == 0:
            k_ref[:, lo:lo + QK_NOPE_DIM] = k_nope
            k_ref[:, lo + QK_NOPE_DIM:lo + HEAD_K] = kr_even
        else:
            k_ref[:, lo:lo + QK_NOPE_DIM] = kr_odd
            k_ref[:, lo + QK_NOPE_DIM:lo + HEAD_K] = k_nope


def _mla_proj(x, positions, g, w_in, q_norm, w_q_up, kv_norm, w_kv_up):
    B, S, D = x.shape
    tile = SEQ_TILE
    n_q, n_k = N_HEADS * HEAD_K, N_HEADS * HEAD_K
    return pl.pallas_call(
        _mla_proj_kernel,
        grid=(B, S // tile),
        in_specs=[
            _row_spec(tile, D),
            _row_spec(tile, 1),
            _const_spec((1, D)),
            _const_spec((D, MLA_H)),
            _const_spec((1, Q_LORA_RANK)),
            _const_spec((Q_LORA_RANK, N_PAIRS * PAIR_Q)),
            _const_spec((1, KV_LORA_RANK)),
            _const_spec((KV_LORA_RANK, n_k)),
            _const_spec((V7X_SUBLANES, V7X_LANES)),
        ],
        out_specs=[_row_spec(tile, n_q), _row_spec(tile, n_k),
                   _row_spec(tile, D_INNER), _row_spec(tile, D_INNER)],
        out_shape=[jax.ShapeDtypeStruct((B, S, n_q), BF16),
                   jax.ShapeDtypeStruct((B, S, n_k), BF16),
                   jax.ShapeDtypeStruct((B, S, D_INNER), BF16),
                   jax.ShapeDtypeStruct((B, S, D_INNER), F32)],
        compiler_params=_params(("arbitrary", "arbitrary")),
        name="mla_proj",
    )(x, positions.reshape(B, S, 1), g.reshape(1, D), w_in, q_norm.reshape(1, -1), w_q_up,
      kv_norm.reshape(1, -1), w_kv_up, jnp.asarray(_rope_constants()))


def _for_each(n, body, unroll):
    def trip(j, carry):
        for u in range(unroll):
            body(j * unroll + u)
        return carry

    def single(i, carry):
        body(i)
        return carry

    full = n // unroll
    lax.fori_loop(0, full, trip, 0)
    lax.fori_loop(full * unroll, n, single, 0)


def _attn_kernel(q_ref, k_ref, v_ref, z_ref, o_ref, s_ref, vx_ref, mrun_ref, mc_ref, acc_ref,
                 *, seq):
    nt = (((1,), (1,)), ((), ()))
    groups = ATTN_TK // V7X_LANES
    lane = lax.broadcasted_iota(jnp.int32, (seq, V_HEAD_DIM), 1)
    vx_ref[:, :V_HEAD_DIM] = v_ref[...]
    vx_ref[:, V_HEAD_DIM:] = jnp.where(lane == 0, 1.0, 0.0).astype(BF16)

    def q_body(qi, carry):
        q0 = pl.multiple_of(qi * ATTN_TQ, ATTN_TQ)
        q = q_ref[pl.ds(q0, ATTN_TQ), :]
        mrun_ref[...] = jnp.full(mrun_ref.shape, MASK_VALUE, F32)

        def scores(kb, diagonal):
            k0 = pl.multiple_of(kb * ATTN_TK, ATTN_TK)
            s = lax.dot_general(q, k_ref[pl.ds(k0, ATTN_TK), :], nt,
                                preferred_element_type=F32)
            if diagonal:
                row = lax.broadcasted_iota(jnp.int32, s.shape, 0)
                col = lax.broadcasted_iota(jnp.int32, s.shape, 1)
                s = jnp.where(col <= row, s, MASK_VALUE)
            s_ref[kb] = s
            m = mrun_ref[...]
            for g in range(groups):
                m = jnp.maximum(m, s[:, g * V7X_LANES:(g + 1) * V7X_LANES])
            mrun_ref[...] = m

        _for_each(qi, functools.partial(scores, diagonal=False), ATTN_UNROLL)
        scores(qi, True)

        row_max = jnp.max(mrun_ref[...], axis=-1, keepdims=True)
        mc_ref[...] = jnp.broadcast_to(row_max * EXP2_SCALE, mc_ref.shape)
        acc_ref[...] = jnp.zeros(acc_ref.shape, F32)

        def weights(kb):
            k0 = pl.multiple_of(kb * ATTN_TK, ATTN_TK)
            mc = mc_ref[...]
            p = jnp.concatenate(
                [jnp.exp2(s_ref[kb, :, g * V7X_LANES:(g + 1) * V7X_LANES] * EXP2_SCALE - mc)
                 .astype(BF16) for g in range(groups)], axis=1)
            acc_ref[...] += _dot(p, vx_ref[pl.ds(k0, ATTN_TK), :])

        _for_each(qi + 1, weights, ATTN_UNROLL)

        acc = acc_ref[...]
        o = acc[:, :V_HEAD_DIM] / acc[:, V_HEAD_DIM:V_HEAD_DIM + 1]
        zz = z_ref[pl.ds(q0, ATTN_TQ), :]
        o_ref[pl.ds(q0, ATTN_TQ), :] = (o * _silu(zz)).astype(BF16)
        return carry

    lax.fori_loop(0, seq // ATTN_TQ, q_body, 0)


def _attention(q, k, v, z):
    B, S, _ = q.shape

    def spec(width):
        return pl.BlockSpec((None, S, width), lambda b, h: (b, 0, h))

    return pl.pallas_call(
        functools.partial(_attn_kernel, seq=S),
        grid=(B, N_HEADS),
        in_specs=[spec(HEAD_K), spec(HEAD_K), spec(V_HEAD_DIM), spec(V_HEAD_DIM)],
        out_specs=spec(V_HEAD_DIM),
        out_shape=jax.ShapeDtypeStruct((B, S, D_INNER), BF16),
        scratch_shapes=[
            pltpu.VMEM((S // ATTN_TK, ATTN_TQ, ATTN_TK), F32),
            pltpu.VMEM((S, 2 * V_HEAD_DIM), BF16),
            pltpu.VMEM((ATTN_TQ, V7X_LANES), F32),
            pltpu.VMEM((ATTN_TQ, V7X_LANES), F32),
            pltpu.VMEM((ATTN_TQ, 2 * V_HEAD_DIM), F32),
        ],
        compiler_params=_params(("arbitrary", "arbitrary")),
        name="mla_attention",
    )(q, k, v, z)


def _out_proj_kernel(x_ref, a_ref, w_ref, o_ref):
    o_ref[...] = x_ref[...] + _dot(a_ref[...], w_ref[...])


def _out_proj(x, a, w_out):
    B, S, D = x.shape
    tile = SEQ_TILE
    return pl.pallas_call(
        _out_proj_kernel,
        grid=(B, S // tile),
        in_specs=[_row_spec(tile, D), _row_spec(tile, D_INNER), _const_spec((D_INNER, D))],
        out_specs=_row_spec(tile, D),
        out_shape=jax.ShapeDtypeStruct((B, S, D), F32),
        compiler_params=_params(("arbitrary", "arbitrary")),
        name="mla_out_proj",
    )(x, a, w_out)


def _mla_layer(x, positions, g, w_in, q_norm, w_q_up, kv_norm, w_kv_up, w_out):
    w_in_cols, q_cols = _mla_column_orders()
    q, k, v, z = _mla_proj(x, positions, g, w_in[:, w_in_cols].astype(BF16), q_norm,
                           w_q_up[:, q_cols].astype(BF16), kv_norm, w_kv_up.astype(BF16))
    gated = _attention(q, k, v, z)
    return _out_proj(x, gated, w_out.astype(BF16))


def kernel(x, positions, pool_norm, pool_w_in, pool_w_grp, pool_scale, pool_w_out,
           conv_norm, conv_w_in, conv_w, conv_w_out,
           mla_norm, mla_w_in, mla_q_norm, mla_w_q_up, mla_kv_norm, mla_w_kv_up, mla_w_out,
           final_norm):
    for i in range(DEPTH):
        kind, j = i % N_MIXERS, i // N_MIXERS
        last = i == DEPTH - 1
        if kind == 0:
            x = _pool_layer(x, pool_norm[j], pool_w_in[j].astype(BF16),
                            pool_w_grp[j].astype(BF16), pool_scale[j],
                            pool_w_out[j].astype(BF16), final_norm if last else None)
        elif kind == 1:
            x = _conv_layer(x, conv_norm[j], conv_w_in[j].astype(BF16), conv_w[j],
                            conv_w_out[j].astype(BF16))
        else:
            x = _mla_layer(x, positions, mla_norm[j], mla_w_in[j], mla_q_norm[j],
                           mla_w_q_up[j], mla_kv_norm[j], mla_w_kv_up[j], mla_w_out[j])
        if last and kind != 0:
            raise NotImplementedError("final norm is fused into a pool layer")
    return x
```

```python
import functools
import math

import numpy as np
import jax
import jax.numpy as jnp
from jax import lax
from jax.experimental import pallas as pl
from jax.experimental.pallas import tpu as pltpu

D_MODEL = 1024
DEPTH = 4
N_MIXERS = 3
D_INNER = 2 * D_MODEL
POOL_WINDOWS = (2, 4, 8, 16)
N_POOL_GROUPS = len(POOL_WINDOWS)
POOL_GROUP = D_INNER // N_POOL_GROUPS
CONV_WIDTH = 3
N_HEADS = 16
N_PAIRS = N_HEADS // 2
QK_NOPE_DIM = 128
QK_ROPE_DIM = 64
ROPE_HALF = QK_ROPE_DIM // 2
V_HEAD_DIM = D_INNER // N_HEADS
Q_LORA_RANK = 384
KV_LORA_RANK = 256
Q_HEAD_DIM = QK_NOPE_DIM + QK_ROPE_DIM
ATTN_SCALE = Q_HEAD_DIM ** -0.5
ROPE_BASE = 10000.0
NORM_EPS = 1e-6
MASK_VALUE = -1e30

V7X_LANES = 128
V7X_SUBLANES = 8
V7X_VMEM_LIMIT_BYTES = 56 * 1024 * 1024

POOL_HALO = max(POOL_WINDOWS)
CONV_HALO = V7X_SUBLANES
SEQ_TILE = 512
CONV_CHUNK = 512
ATTN_TQ = 512
ATTN_CHUNK = 1024
EXP2_SCALE = ATTN_SCALE * math.log2(math.e)
PAIR_Q = 2 * QK_NOPE_DIM + 2 * QK_ROPE_DIM
HEAD_K = 2 * QK_NOPE_DIM
MLA_H = Q_LORA_RANK + KV_LORA_RANK + V7X_LANES + D_INNER

F32 = jnp.float32
BF16 = jnp.bfloat16


def _rms_scale(x, g):
    ms = jnp.mean(x * x, axis=-1, keepdims=True)
    return x * lax.rsqrt(ms + NORM_EPS) * g


def _silu(z):
    return z * (1.0 / (1.0 + jnp.exp(-z)))


def _dot(a, b):
    return jnp.dot(a, b, preferred_element_type=F32)


def _const_spec(shape):
    zeros = (0,) * len(shape)
    return pl.BlockSpec(shape, lambda *_: zeros, pipeline_mode=pl.Buffered(1))


def _row_spec(tile, width):
    return pl.BlockSpec((None, tile, width), lambda b, s: (b, s, 0))


def _params(semantics):
    return pltpu.CompilerParams(dimension_semantics=semantics,
                                vmem_limit_bytes=V7X_VMEM_LIMIT_BYTES)


def _pool_kernel(*refs, tile, final):
    if final:
        x_ref, g_ref, win_ref, wgrp_ref, scale_ref, wout_ref, fin_ref, o_ref, tail_ref = refs
    else:
        x_ref, g_ref, win_ref, wgrp_ref, scale_ref, wout_ref, o_ref, tail_ref = refs
    s_idx = pl.program_id(1)

    @pl.when(s_idx == 0)
    def _():
        tail_ref[...] = jnp.zeros_like(tail_ref)

    x = x_ref[...]
    xn = _rms_scale(x, g_ref[...]).astype(BF16)
    pos = s_idx * tile + lax.broadcasted_iota(jnp.int32, (tile, 1), 0)
    count = (pos + 1).astype(F32)
    acc = None
    for g, w in enumerate(POOL_WINDOWS):
        lo, hi = g * POOL_GROUP, (g + 1) * POOL_GROUP
        u = _dot(xn, win_ref[:, lo:hi])
        z = _dot(xn, win_ref[:, D_INNER + lo:D_INNER + hi])
        s = jnp.concatenate([tail_ref[:, lo:hi], u], axis=0)
        k = 1
        while k < w:
            s = s + pltpu.roll(s, k, axis=0)
            k *= 2
        s = s[POOL_HALO:, :]
        inv = 1.0 / jnp.minimum(count, float(w))
        pooled = (s * inv - u).astype(BF16)
        mixed = _dot(pooled, wgrp_ref[g]) * scale_ref[:, lo:hi]
        gated = (mixed * _silu(z)).astype(BF16)
        part = _dot(gated, wout_ref[lo:hi, :])
        acc = part if acc is None else acc + part
        tail_ref[:, lo:hi] = u[tile - POOL_HALO:, :]
    y = x + acc
    if final:
        y = _rms_scale(y, fin_ref[...])
    o_ref[...] = y


def _pool_layer(x, g, w_in, w_grp, scale, w_out, final_g=None):
    B, S, D = x.shape
    tile = SEQ_TILE
    final = final_g is not None
    in_specs = [
        _row_spec(tile, D),
        _const_spec((1, D)),
        _const_spec((D, 2 * D_INNER)),
        _const_spec((N_POOL_GROUPS, POOL_GROUP, POOL_GROUP)),
        _const_spec((1, D_INNER)),
        _const_spec((D_INNER, D)),
    ]
    args = [x, g.reshape(1, D), w_in, w_grp, scale.reshape(1, D_INNER), w_out]
    if final:
        in_specs.append(_const_spec((1, D)))
        args.append(final_g.reshape(1, D))
    return pl.pallas_call(
        functools.partial(_pool_kernel, tile=tile, final=final),
        grid=(B, S // tile),
        in_specs=in_specs,
        out_specs=_row_spec(tile, D),
        out_shape=jax.ShapeDtypeStruct((B, S, D), F32),
        scratch_shapes=[pltpu.VMEM((POOL_HALO, D_INNER), F32)],
        compiler_params=_params(("arbitrary", "arbitrary")),
        name="pool_layer",
    )(*args)


def _conv_kernel(x_ref, g_ref, win_ref, cw_ref, wout_ref, o_ref, tail_ref, *, tile):
    s_idx = pl.program_id(1)

    @pl.when(s_idx == 0)
    def _():
        tail_ref[...] = jnp.zeros_like(tail_ref)

    x = x_ref[...]
    xn = _rms_scale(x, g_ref[...]).astype(BF16)
    acc = None
    for c in range(D_INNER // CONV_CHUNK):
        lo, hi = c * CONV_CHUNK, (c + 1) * CONV_CHUNK
        bb = _dot(xn, win_ref[:, lo:hi])
        cc = _dot(xn, win_ref[:, D_INNER + lo:D_INNER + hi])
        hh = _dot(xn, win_ref[:, 2 * D_INNER + lo:2 * D_INNER + hi])
        zz = _dot(xn, win_ref[:, 3 * D_INNER + lo:3 * D_INNER + hi])
        ch = cc * hh
        che = jnp.concatenate([tail_ref[:, lo:hi], ch], axis=0)
        cw = cw_ref[:, lo:hi]
        conv = (che * cw[2:3, :] + pltpu.roll(che, 1, axis=0) * cw[1:2, :]
                + pltpu.roll(che, 2, axis=0) * cw[0:1, :])[CONV_HALO:, :]
        y = (bb * conv * _silu(zz)).astype(BF16)
        part = _dot(y, wout_ref[lo:hi, :])
        acc = part if acc is None else acc + part
        tail_ref[:, lo:hi] = ch[tile - CONV_HALO:, :]
    o_ref[...] = x + acc


def _conv_layer(x, g, w_in, conv_w, w_out):
    B, S, D = x.shape
    tile = SEQ_TILE
    return pl.pallas_call(
        functools.partial(_conv_kernel, tile=tile),
        grid=(B, S // tile),
        in_specs=[
            _row_spec(tile, D),
            _const_spec((1, D)),
            _const_spec((D, 4 * D_INNER)),
            _const_spec((CONV_WIDTH, D_INNER)),
            _const_spec((D_INNER, D)),
        ],
        out_specs=_row_spec(tile, D),
        out_shape=jax.ShapeDtypeStruct((B, S, D), F32),
        scratch_shapes=[pltpu.VMEM((CONV_HALO, D_INNER), F32)],
        compiler_params=_params(("arbitrary", "arbitrary")),
        name="conv_layer",
    )(x, g.reshape(1, D), w_in, conv_w, w_out)


def _rope_constants():
    inv_freq = ROPE_BASE ** (-np.arange(0, QK_ROPE_DIM, 2, dtype=np.float32) / QK_ROPE_DIM)
    lane_group = np.arange(V7X_LANES) // ROPE_HALF
    rows = np.zeros((V7X_SUBLANES, V7X_LANES), np.float32)
    rows[0] = np.tile(inv_freq.astype(np.float32), V7X_LANES // ROPE_HALF)
    rows[1] = np.where(lane_group < 2, -1.0, 1.0)
    rows[2] = (lane_group % 2 == 0)
    rows[3] = (lane_group % 2 == 1)
    return rows


def _mla_column_orders():
    half = np.arange(ROPE_HALF)
    kr0 = Q_LORA_RANK + KV_LORA_RANK
    w_in_cols = np.concatenate([
        np.arange(kr0),
        kr0 + half, kr0 + half, kr0 + ROPE_HALF + half, kr0 + ROPE_HALF + half,
        np.arange(kr0 + QK_ROPE_DIM, kr0 + QK_ROPE_DIM + D_INNER)])
    q_cols = []
    for p in range(N_PAIRS):
        h0, h1 = 2 * p * Q_HEAD_DIM, (2 * p + 1) * Q_HEAD_DIM
        nope = np.arange(QK_NOPE_DIM)
        q_cols += [h0 + nope,
                   h0 + QK_NOPE_DIM + half, h1 + QK_NOPE_DIM + half,
                   h0 + QK_NOPE_DIM + ROPE_HALF + half, h1 + QK_NOPE_DIM + ROPE_HALF + half,
                   h1 + nope]
    return w_in_cols, np.concatenate(q_cols)


def _mla_proj_kernel(x_ref, pos_ref, g_ref, win_ref, qn_ref, wq_ref, kvn_ref, wkv_ref, rc_ref,
                     q_ref, k_ref, v_ref, z_ref):
    xn = _rms_scale(x_ref[...], g_ref[...]).astype(BF16)
    kv0 = Q_LORA_RANK
    kr0 = kv0 + KV_LORA_RANK
    z0 = kr0 + V7X_LANES
    q_lat = _dot(xn, win_ref[:, :kv0])
    kv_lat = _dot(xn, win_ref[:, kv0:kr0])
    kr = _dot(xn, win_ref[:, kr0:z0])
    z_ref[...] = _dot(xn, win_ref[:, z0:])

    angles = pos_ref[...].astype(F32) * rc_ref[0:1, :]
    cos = jnp.cos(angles)
    sin = jnp.sin(angles) * rc_ref[1:2, :]

    def rope(t):
        return t * cos + pltpu.roll(t, 2 * ROPE_HALF, axis=1) * sin

    qn = _rms_scale(q_lat, qn_ref[...]).astype(BF16)
    for p in range(N_PAIRS):
        qp = _dot(qn, wq_ref[:, p * PAIR_Q:(p + 1) * PAIR_Q])
        q_rope = rope(qp[:, QK_NOPE_DIM:2 * QK_NOPE_DIM]).astype(BF16)
        lo = 2 * p * HEAD_K
        q_ref[:, lo:lo + QK_NOPE_DIM] = qp[:, :QK_NOPE_DIM].astype(BF16)
        q_ref[:, lo + QK_NOPE_DIM:lo + HEAD_K] = q_rope
        q_ref[:, lo + HEAD_K:lo + HEAD_K + QK_NOPE_DIM] = q_rope
        q_ref[:, lo + HEAD_K + QK_NOPE_DIM:lo + 2 * HEAD_K] = qp[:, 2 * QK_NOPE_DIM:].astype(BF16)

    kr = rope(kr)
    kr_even = (kr * rc_ref[2:3, :]).astype(BF16)
    kr_odd = (kr * rc_ref[3:4, :]).astype(BF16)
    kvn = _rms_scale(kv_lat, kvn_ref[...]).astype(BF16)
    for h in range(N_HEADS):
        lo = h * HEAD_K
        kv = _dot(kvn, wkv_ref[:, lo:lo + HEAD_K])
        k_nope = kv[:, :QK_NOPE_DIM].astype(BF16)
        v_ref[:, h * V_HEAD_DIM:(h + 1) * V_HEAD_DIM] = kv[:, QK_NOPE_DIM:].astype(BF16)
        if h % 2 == 0:
            k_ref[:, lo:lo + QK_NOPE_DIM] = k_nope
            k_ref[:, lo + QK_NOPE_DIM:lo + HEAD_K] = kr_even
        else:
            k_ref[:, lo:lo + QK_NOPE_DIM] = kr_odd
            k_ref[:, lo + QK_NOPE_DIM:lo + HEAD_K] = k_nope


def _mla_proj(x, positions, g, w_in, q_norm, w_q_up, kv_norm, w_kv_up):
    B, S, D = x.shape
    tile = SEQ_TILE
    n_q, n_k = N_HEADS * HEAD_K, N_HEADS * HEAD_K
    return pl.pallas_call(
        _mla_proj_kernel,
        grid=(B, S // tile),
        in_specs=[
            _row_spec(tile, D),
            _row_spec(tile, 1),
            _const_spec((1, D)),
            _const_spec((D, MLA_H)),
            _const_spec((1, Q_LORA_RANK)),
            _const_spec((Q_LORA_RANK, N_PAIRS * PAIR_Q)),
            _const_spec((1, KV_LORA_RANK)),
            _const_spec((KV_LORA_RANK, n_k)),
            _const_spec((V7X_SUBLANES, V7X_LANES)),
        ],
        out_specs=[_row_spec(tile, n_q), _row_spec(tile, n_k),
                   _row_spec(tile, D_INNER), _row_spec(tile, D_INNER)],
        out_shape=[jax.ShapeDtypeStruct((B, S, n_q), BF16),
                   jax.ShapeDtypeStruct((B, S, n_k), BF16),
                   jax.ShapeDtypeStruct((B, S, D_INNER), BF16),
                   jax.ShapeDtypeStruct((B, S, D_INNER), F32)],
        compiler_params=_params(("arbitrary", "arbitrary")),
        name="mla_proj",
    )(x, positions.reshape(B, S, 1), g.reshape(1, D), w_in, q_norm.reshape(1, -1), w_q_up,
      kv_norm.reshape(1, -1), w_kv_up, jnp.asarray(_rope_constants()))


def _lane_group_max(m, s):
    for g0 in range(0, s.shape[1], V7X_LANES):
        piece = s[:, g0:g0 + V7X_LANES]
        m = piece if m is None else jnp.maximum(m, piece)
    return m


def _attn_kernel(q_ref, k_ref, v_ref, z_ref, o_ref, s_ref, vx_ref, *, seq):
    nt = (((1,), (1,)), ((), ()))
    n_tiles = seq // ATTN_TQ
    lane = lax.broadcasted_iota(jnp.int32, (seq, V_HEAD_DIM), 1)
    vx_ref[:, :V_HEAD_DIM] = v_ref[...]
    vx_ref[:, V_HEAD_DIM:] = jnp.where(lane == 0, 1.0, 0.0).astype(BF16)

    def scores(i):
        buf, lo, hi = i % 2, i * ATTN_TQ, (i + 1) * ATTN_TQ
        q = q_ref[lo:hi, :]
        m = None
        for c0 in range(0, lo, ATTN_CHUNK):
            c1 = min(c0 + ATTN_CHUNK, lo)
            s = lax.dot_general(q, k_ref[c0:c1, :], nt, preferred_element_type=F32)
            s_ref[buf, :, c0:c1] = s
            m = _lane_group_max(m, s)
        s = lax.dot_general(q, k_ref[lo:hi, :], nt, preferred_element_type=F32)
        row = lax.broadcasted_iota(jnp.int32, s.shape, 0)
        col = lax.broadcasted_iota(jnp.int32, s.shape, 1)
        s = jnp.where(col <= row, s, MASK_VALUE)
        s_ref[buf, :, lo:hi] = s
        m = _lane_group_max(m, s)
        return jnp.max(m, axis=-1, keepdims=True) * EXP2_SCALE

    def weights(i, mc):
        buf, lo, hi = i % 2, i * ATTN_TQ, (i + 1) * ATTN_TQ
        mcb = jnp.broadcast_to(mc, (ATTN_TQ, V7X_LANES))
        acc = None
        for c0 in range(0, hi, ATTN_CHUNK):
            c1 = min(c0 + ATTN_CHUNK, hi)
            p = jnp.concatenate(
                [jnp.exp2(s_ref[buf, :, g0:g0 + V7X_LANES] * EXP2_SCALE - mcb).astype(BF16)
                 for g0 in range(c0, c1, V7X_LANES)], axis=1)
            part = _dot(p, vx_ref[c0:c1, :])
            acc = part if acc is None else acc + part
        o = acc[:, :V_HEAD_DIM] / acc[:, V_HEAD_DIM:V_HEAD_DIM + 1]
        o_ref[lo:hi, :] = (o * _silu(z_ref[lo:hi, :])).astype(BF16)

    mc = scores(0)
    for i in range(n_tiles):
        mc_next = scores(i + 1) if i + 1 < n_tiles else None
        weights(i, mc)
        mc = mc_next


def _attention(q, k, v, z):
    B, S, _ = q.shape

    def spec(width):
        return pl.BlockSpec((None, S, width), lambda b, h: (b, 0, h))

    return pl.pallas_call(
        functools.partial(_attn_kernel, seq=S),
        grid=(B, N_HEADS),
        in_specs=[spec(HEAD_K), spec(HEAD_K), spec(V_HEAD_DIM), spec(V_HEAD_DIM)],
        out_specs=spec(V_HEAD_DIM),
        out_shape=jax.ShapeDtypeStruct((B, S, D_INNER), BF16),
        scratch_shapes=[
            pltpu.VMEM((2, ATTN_TQ, S), F32),
            pltpu.VMEM((S, 2 * V_HEAD_DIM), BF16),
        ],
        compiler_params=_params(("arbitrary", "arbitrary")),
        name="mla_attention",
    )(q, k, v, z)


def _out_proj_kernel(x_ref, a_ref, w_ref, o_ref):
    o_ref[...] = x_ref[...] + _dot(a_ref[...], w_ref[...])


def _out_proj(x, a, w_out):
    B, S, D = x.shape
    tile = SEQ_TILE
    return pl.pallas_call(
        _out_proj_kernel,
        grid=(B, S // tile),
        in_specs=[_row_spec(tile, D), _row_spec(tile, D_INNER), _const_spec((D_INNER, D))],
        out_specs=_row_spec(tile, D),
        out_shape=jax.ShapeDtypeStruct((B, S, D), F32),
        compiler_params=_params(("arbitrary", "arbitrary")),
        name="mla_out_proj",
    )(x, a, w_out)


def _mla_layer(x, positions, g, w_in, q_norm, w_q_up, kv_norm, w_kv_up, w_out):
    w_in_cols, q_cols = _mla_column_orders()
    q, k, v, z = _mla_proj(x, positions, g, w_in[:, w_in_cols].astype(BF16), q_norm,
                           w_q_up[:, q_cols].astype(BF16), kv_norm, w_kv_up.astype(BF16))
    gated = _attention(q, k, v, z)
    return _out_proj(x, gated, w_out.astype(BF16))


def kernel(x, positions, pool_norm, pool_w_in, pool_w_grp, pool_scale, pool_w_out,
           conv_norm, conv_w_in, conv_w, conv_w_out,
           mla_norm, mla_w_in, mla_q_norm, mla_w_q_up, mla_kv_norm, mla_w_kv_up, mla_w_out,
           final_norm):
    for i in range(DEPTH):
        kind, j = i % N_MIXERS, i // N_MIXERS
        last = i == DEPTH - 1
        if kind == 0:
            x = _pool_layer(x, pool_norm[j], pool_w_in[j].astype(BF16),
                            pool_w_grp[j].astype(BF16), pool_scale[j],
                            pool_w_out[j].astype(BF16), final_norm if last else None)
        elif kind == 1:
            x = _conv_layer(x, conv_norm[j], conv_w_in[j].astype(BF16), conv_w[j],
                            conv_w_out[j].astype(BF16))
        else:
            x = _mla_layer(x, positions, mla_norm[j], mla_w_in[j], mla_q_norm[j],
                           mla_w_q_up[j], mla_kv_norm[j], mla_w_kv_up[j], mla_w_out[j])
        if last and kind != 0:
            raise NotImplementedError("final norm is fused into a pool layer")
    return x
```

```python
import functools
import math

import numpy as np
import jax
import jax.numpy as jnp
from jax import lax
from jax.experimental import pallas as pl
from jax.experimental.pallas import tpu as pltpu

D_MODEL = 1024
DEPTH = 4
N_MIXERS = 3
D_INNER = 2 * D_MODEL
POOL_WINDOWS = (2, 4, 8, 16)
N_POOL_GROUPS = len(POOL_WINDOWS)
POOL_GROUP = D_INNER // N_POOL_GROUPS
CONV_WIDTH = 3
N_HEADS = 16
N_PAIRS = N_HEADS // 2
QK_NOPE_DIM = 128
QK_ROPE_DIM = 64
ROPE_HALF = QK_ROPE_DIM // 2
V_HEAD_DIM = D_INNER // N_HEADS
Q_LORA_RANK = 384
KV_LORA_RANK = 256
Q_HEAD_DIM = QK_NOPE_DIM + QK_ROPE_DIM
ATTN_SCALE = Q_HEAD_DIM ** -0.5
ROPE_BASE = 10000.0
NORM_EPS = 1e-6
MASK_VALUE = -1e30

V7X_LANES = 128
V7X_SUBLANES = 8
V7X_BF16_SUBLANES = 16
V7X_VMEM_LIMIT_BYTES = 56 * 1024 * 1024

POOL_HALO = max(POOL_WINDOWS)
CONV_HALO = V7X_SUBLANES
SEQ_TILE = 512
OUT_PROJ_TILE = 1024
CONV_CHUNK = 512
ATTN_TQ = 512
ATTN_CHUNK = 4096
EXP2_SCALE = ATTN_SCALE * math.log2(math.e)
PAIR_Q = 2 * QK_NOPE_DIM + 2 * QK_ROPE_DIM
HEAD_K = 2 * QK_NOPE_DIM
MLA_H = Q_LORA_RANK + KV_LORA_RANK + V7X_LANES + D_INNER

F32 = jnp.float32
BF16 = jnp.bfloat16


def _rms_scale(x, g):
    ms = jnp.mean(x * x, axis=-1, keepdims=True)
    return x * lax.rsqrt(ms + NORM_EPS) * g


def _silu(z):
    return z * (1.0 / (1.0 + jnp.exp(-z)))


def _dot(a, b):
    return jnp.dot(a, b, preferred_element_type=F32)


def _const_spec(shape):
    zeros = (0,) * len(shape)
    return pl.BlockSpec(shape, lambda *_: zeros, pipeline_mode=pl.Buffered(1))


def _row_spec(tile, width):
    return pl.BlockSpec((None, tile, width), lambda b, s: (b, s, 0))


def _params(semantics):
    return pltpu.CompilerParams(dimension_semantics=semantics,
                                vmem_limit_bytes=V7X_VMEM_LIMIT_BYTES)


def _pool_kernel(*refs, tile, final):
    if final:
        x_ref, g_ref, win_ref, wgrp_ref, scale_ref, wout_ref, fin_ref, o_ref, tail_ref = refs
    else:
        x_ref, g_ref, win_ref, wgrp_ref, scale_ref, wout_ref, o_ref, tail_ref = refs
    s_idx = pl.program_id(1)

    @pl.when(s_idx == 0)
    def _():
        tail_ref[...] = jnp.zeros_like(tail_ref)

    x = x_ref[...]
    xn = _rms_scale(x, g_ref[...]).astype(BF16)
    pos = s_idx * tile + lax.broadcasted_iota(jnp.int32, (tile, 1), 0)
    count = (pos + 1).astype(F32)
    acc = None
    for g, w in enumerate(POOL_WINDOWS):
        lo, hi = g * POOL_GROUP, (g + 1) * POOL_GROUP
        u = _dot(xn, win_ref[:, lo:hi])
        z = _dot(xn, win_ref[:, D_INNER + lo:D_INNER + hi])
        s = jnp.concatenate([tail_ref[:, lo:hi], u], axis=0)
        k = 1
        while k < w:
            s = s + pltpu.roll(s, k, axis=0)
            k *= 2
        s = s[POOL_HALO:, :]
        inv = 1.0 / jnp.minimum(count, float(w))
        pooled = (s * inv - u).astype(BF16)
        mixed = _dot(pooled, wgrp_ref[g]) * scale_ref[:, lo:hi]
        gated = (mixed * _silu(z)).astype(BF16)
        part = _dot(gated, wout_ref[lo:hi, :])
        acc = part if acc is None else acc + part
        tail_ref[:, lo:hi] = u[tile - POOL_HALO:, :]
    y = x + acc
    if final:
        y = _rms_scale(y, fin_ref[...])
    o_ref[...] = y


def _pool_layer(x, g, w_in, w_grp, scale, w_out, final_g=None):
    B, S, D = x.shape
    tile = SEQ_TILE
    final = final_g is not None
    in_specs = [
        _row_spec(tile, D),
        _const_spec((1, D)),
        _const_spec((D, 2 * D_INNER)),
        _const_spec((N_POOL_GROUPS, POOL_GROUP, POOL_GROUP)),
        _const_spec((1, D_INNER)),
        _const_spec((D_INNER, D)),
    ]
    args = [x, g.reshape(1, D), w_in, w_grp, scale.reshape(1, D_INNER), w_out]
    if final:
        in_specs.append(_const_spec((1, D)))
        args.append(final_g.reshape(1, D))
    return pl.pallas_call(
        functools.partial(_pool_kernel, tile=tile, final=final),
        grid=(B, S // tile),
        in_specs=in_specs,
        out_specs=_row_spec(tile, D),
        out_shape=jax.ShapeDtypeStruct((B, S, D), F32),
        scratch_shapes=[pltpu.VMEM((POOL_HALO, D_INNER), F32)],
        compiler_params=_params(("arbitrary", "arbitrary")),
        name="pool_layer",
    )(*args)


def _conv_kernel(x_ref, g_ref, win_ref, cw_ref, wout_ref, o_ref, tail_ref, *, tile):
    s_idx = pl.program_id(1)

    @pl.when(s_idx == 0)
    def _():
        tail_ref[...] = jnp.zeros_like(tail_ref)

    x = x_ref[...]
    xn = _rms_scale(x, g_ref[...]).astype(BF16)
    acc = None
    for c in range(D_INNER // CONV_CHUNK):
        lo, hi = c * CONV_CHUNK, (c + 1) * CONV_CHUNK
        bb = _dot(xn, win_ref[:, lo:hi])
        cc = _dot(xn, win_ref[:, D_INNER + lo:D_INNER + hi])
        hh = _dot(xn, win_ref[:, 2 * D_INNER + lo:2 * D_INNER + hi])
        zz = _dot(xn, win_ref[:, 3 * D_INNER + lo:3 * D_INNER + hi])
        ch = cc * hh
        che = jnp.concatenate([tail_ref[:, lo:hi], ch], axis=0)
        cw = cw_ref[:, lo:hi]
        conv = (che * cw[2:3, :] + pltpu.roll(che, 1, axis=0) * cw[1:2, :]
                + pltpu.roll(che, 2, axis=0) * cw[0:1, :])[CONV_HALO:, :]
        y = (bb * conv * _silu(zz)).astype(BF16)
        part = _dot(y, wout_ref[lo:hi, :])
        acc = part if acc is None else acc + part
        tail_ref[:, lo:hi] = ch[tile - CONV_HALO:, :]
    o_ref[...] = x + acc


def _conv_layer(x, g, w_in, conv_w, w_out):
    B, S, D = x.shape
    tile = SEQ_TILE
    return pl.pallas_call(
        functools.partial(_conv_kernel, tile=tile),
        grid=(B, S // tile),
        in_specs=[
            _row_spec(tile, D),
            _const_spec((1, D)),
            _const_spec((D, 4 * D_INNER)),
            _const_spec((CONV_WIDTH, D_INNER)),
            _const_spec((D_INNER, D)),
        ],
        out_specs=_row_spec(tile, D),
        out_shape=jax.ShapeDtypeStruct((B, S, D), F32),
        scratch_shapes=[pltpu.VMEM((CONV_HALO, D_INNER), F32)],
        compiler_params=_params(("arbitrary", "arbitrary")),
        name="conv_layer",
    )(x, g.reshape(1, D), w_in, conv_w, w_out)


def _rope_constants():
    inv_freq = ROPE_BASE ** (-np.arange(0, QK_ROPE_DIM, 2, dtype=np.float32) / QK_ROPE_DIM)
    lane_group = np.arange(V7X_LANES) // ROPE_HALF
    rows = np.zeros((V7X_SUBLANES, V7X_LANES), np.float32)
    rows[0] = np.tile(inv_freq.astype(np.float32), V7X_LANES // ROPE_HALF)
    rows[1] = np.where(lane_group < 2, -1.0, 1.0)
    rows[2] = (lane_group % 2 == 0)
    rows[3] = (lane_group % 2 == 1)
    return rows


def _mla_column_pieces():
    kr0 = Q_LORA_RANK + KV_LORA_RANK
    w_in_pieces = [(0, kr0),
                   (kr0, ROPE_HALF), (kr0, ROPE_HALF),
                   (kr0 + ROPE_HALF, ROPE_HALF), (kr0 + ROPE_HALF, ROPE_HALF),
                   (kr0 + QK_ROPE_DIM, D_INNER)]
    q_pieces = []
    for p in range(N_PAIRS):
        h0, h1 = 2 * p * Q_HEAD_DIM, (2 * p + 1) * Q_HEAD_DIM
        q_pieces += [(h0, QK_NOPE_DIM),
                     (h0 + QK_NOPE_DIM, ROPE_HALF), (h1 + QK_NOPE_DIM, ROPE_HALF),
                     (h0 + QK_NOPE_DIM + ROPE_HALF, ROPE_HALF),
                     (h1 + QK_NOPE_DIM + ROPE_HALF, ROPE_HALF),
                     (h1, QK_NOPE_DIM)]
    return w_in_pieces, q_pieces


def _take_columns(w, pieces):
    return jnp.concatenate([w[:, a:a + n] for a, n in pieces], axis=1).astype(BF16)


def _mla_proj_kernel(x_ref, pos_ref, g_ref, win_ref, qn_ref, wq_ref, kvn_ref, wkv_ref, rc_ref,
                     q_ref, k_ref, v_ref, z_ref):
    xn = _rms_scale(x_ref[...], g_ref[...]).astype(BF16)
    kv0 = Q_LORA_RANK
    kr0 = kv0 + KV_LORA_RANK
    z0 = kr0 + V7X_LANES
    q_lat = _dot(xn, win_ref[:, :kv0])
    kv_lat = _dot(xn, win_ref[:, kv0:kr0])
    kr = _dot(xn, win_ref[:, kr0:z0])
    z_ref[...] = _dot(xn, win_ref[:, z0:])

    angles = pos_ref[...].astype(F32) * rc_ref[0:1, :]
    cos = jnp.cos(angles)
    sin = jnp.sin(angles) * rc_ref[1:2, :]

    def rope(t):
        return t * cos + pltpu.roll(t, 2 * ROPE_HALF, axis=1) * sin

    qn = _rms_scale(q_lat, qn_ref[...]).astype(BF16)
    for p in range(N_PAIRS):
        qp = _dot(qn, wq_ref[:, p * PAIR_Q:(p + 1) * PAIR_Q]) * EXP2_SCALE
        q_rope = rope(qp[:, QK_NOPE_DIM:2 * QK_NOPE_DIM]).astype(BF16)
        lo = 2 * p * HEAD_K
        q_ref[:, lo:lo + QK_NOPE_DIM] = qp[:, :QK_NOPE_DIM].astype(BF16)
        q_ref[:, lo + QK_NOPE_DIM:lo + HEAD_K] = q_rope
        q_ref[:, lo + HEAD_K:lo + HEAD_K + QK_NOPE_DIM] = q_rope
        q_ref[:, lo + HEAD_K + QK_NOPE_DIM:lo + 2 * HEAD_K] = qp[:, 2 * QK_NOPE_DIM:].astype(BF16)

    kr = rope(kr)
    kr_even = (kr * rc_ref[2:3, :]).astype(BF16)
    kr_odd = (kr * rc_ref[3:4, :]).astype(BF16)
    kvn = _rms_scale(kv_lat, kvn_ref[...]).astype(BF16)
    for h in range(N_HEADS):
        lo = h * HEAD_K
        kv = _dot(kvn, wkv_ref[:, lo:lo + HEAD_K])
        k_nope = kv[:, :QK_NOPE_DIM].astype(BF16)
        v_ref[:, h * V_HEAD_DIM:(h + 1) * V_HEAD_DIM] = kv[:, QK_NOPE_DIM:].astype(BF16)
        if h % 2 == 0:
            k_ref[:, lo:lo + QK_NOPE_DIM] = k_nope
            k_ref[:, lo + QK_NOPE_DIM:lo + HEAD_K] = kr_even
        else:
            k_ref[:, lo:lo + QK_NOPE_DIM] = kr_odd
            k_ref[:, lo + QK_NOPE_DIM:lo + HEAD_K] = k_nope


def _mla_proj(x, positions, g, w_in, q_norm, w_q_up, kv_norm, w_kv_up):
    B, S, D = x.shape
    tile = SEQ_TILE
    n_q, n_k = N_HEADS * HEAD_K, N_HEADS * HEAD_K
    return pl.pallas_call(
        _mla_proj_kernel,
        grid=(B, S // tile),
        in_specs=[
            _row_spec(tile, D),
            _row_spec(tile, 1),
            _const_spec((1, D)),
            _const_spec((D, MLA_H)),
            _const_spec((1, Q_LORA_RANK)),
            _const_spec((Q_LORA_RANK, N_PAIRS * PAIR_Q)),
            _const_spec((1, KV_LORA_RANK)),
            _const_spec((KV_LORA_RANK, n_k)),
            _const_spec((V7X_SUBLANES, V7X_LANES)),
        ],
        out_specs=[_row_spec(tile, n_q), _row_spec(tile, n_k),
                   _row_spec(tile, D_INNER), _row_spec(tile, D_INNER)],
        out_shape=[jax.ShapeDtypeStruct((B, S, n_q), BF16),
                   jax.ShapeDtypeStruct((B, S, n_k), BF16),
                   jax.ShapeDtypeStruct((B, S, D_INNER), BF16),
                   jax.ShapeDtypeStruct((B, S, D_INNER), F32)],
        compiler_params=_params(("arbitrary", "arbitrary")),
        name="mla_proj",
    )(x, positions.reshape(B, S, 1), g.reshape(1, D), w_in, q_norm.reshape(1, -1), w_q_up,
      kv_norm.reshape(1, -1), w_kv_up, jnp.asarray(_rope_constants()))


def _attn_kernel(q_ref, k_ref, v_ref, z_ref, o_ref, s_ref, vxt_ref, *, seq):
    nt = (((1,), (1,)), ((), ()))
    n_tiles = seq // ATTN_TQ
    half = ATTN_TQ // 2
    vxt_ref[:V_HEAD_DIM, :] = v_ref[...].astype(F32).T.astype(BF16)
    pad_row = lax.broadcasted_iota(jnp.int32, (V7X_BF16_SUBLANES, seq), 0)
    vxt_ref[V_HEAD_DIM:, :] = jnp.where(pad_row == 0, 1.0, 0.0).astype(BF16)

    def causal(s, first_key, first_query):
        key = first_key + lax.broadcasted_iota(jnp.int32, s.shape, 0)
        query = first_query + lax.broadcasted_iota(jnp.int32, s.shape, 1)
        return jnp.where(key <= query, s, MASK_VALUE)

    def col_max(m, s):
        top = jnp.max(s, axis=0, keepdims=True)
        return top if m is None else jnp.maximum(m, top)

    def scores(i):
        buf, lo, hi = i % 2, i * ATTN_TQ, (i + 1) * ATTN_TQ
        mid = lo + half
        q = q_ref[lo:hi, :]
        m = None
        for c0 in range(0, lo, ATTN_CHUNK):
            c1 = min(c0 + ATTN_CHUNK, lo)
            s = lax.dot_general(k_ref[c0:c1, :], q, nt, preferred_element_type=F32)
            s_ref[buf, c0:c1, :] = s
            m = col_max(m, s)
        s = causal(lax.dot_general(k_ref[lo:mid, :], q, nt, preferred_element_type=F32), lo, lo)
        s_ref[buf, lo:mid, :] = s
        m = col_max(m, s)
        s = causal(lax.dot_general(k_ref[mid:hi, :], q[half:, :], nt,
                                   preferred_element_type=F32), mid, mid)
        s_ref[buf, mid:hi, half:] = s
        m = jnp.concatenate([m[:, :half], col_max(m[:, half:], s)], axis=1)
        return m

    def weights(i, mc):
        buf, lo, hi = i % 2, i * ATTN_TQ, (i + 1) * ATTN_TQ
        mid = lo + half

        def probs(r0, r1, c0, c1):
            return jnp.exp2(s_ref[buf, r0:r1, c0:c1] - mc[:, c0:c1]).astype(BF16)

        acc = None
        for c0 in range(0, mid, ATTN_CHUNK):
            c1 = min(c0 + ATTN_CHUNK, mid)
            part = _dot(vxt_ref[:, c0:c1], probs(c0, c1, 0, ATTN_TQ))
            acc = part if acc is None else acc + part
        late = _dot(vxt_ref[:, mid:hi], probs(mid, hi, half, ATTN_TQ))
        acc = jnp.concatenate([acc[:, :half], acc[:, half:] + late], axis=1)
        o_t = acc[:V_HEAD_DIM, :] / acc[V_HEAD_DIM:V_HEAD_DIM + 1, :]
        o_ref[lo:hi, :] = (o_t.T * _silu(z_ref[lo:hi, :])).astype(BF16)

    mc = scores(0)
    for i in range(n_tiles):
        mc_next = scores(i + 1) if i + 1 < n_tiles else None
        weights(i, mc)
        mc = mc_next


def _attention(q, k, v, z):
    B, S, _ = q.shape

    def spec(width):
        return pl.BlockSpec((None, S, width), lambda b, h: (b, 0, h))

    return pl.pallas_call(
        functools.partial(_attn_kernel, seq=S),
        grid=(B, N_HEADS),
        in_specs=[spec(HEAD_K), spec(HEAD_K), spec(V_HEAD_DIM), spec(V_HEAD_DIM)],
        out_specs=spec(V_HEAD_DIM),
        out_shape=jax.ShapeDtypeStruct((B, S, D_INNER), BF16),
        scratch_shapes=[
            pltpu.VMEM((2, S, ATTN_TQ), F32),
            pltpu.VMEM((V_HEAD_DIM + V7X_BF16_SUBLANES, S), BF16),
        ],
        compiler_params=_params(("arbitrary", "arbitrary")),
        name="mla_attention",
    )(q, k, v, z)


def _out_proj_kernel(x_ref, a_ref, w_ref, o_ref):
    o_ref[...] = x_ref[...] + _dot(a_ref[...], w_ref[...])


def _out_proj(x, a, w_out):
    B, S, D = x.shape
    tile = OUT_PROJ_TILE
    return pl.pallas_call(
        _out_proj_kernel,
        grid=(B, S // tile),
        in_specs=[_row_spec(tile, D), _row_spec(tile, D_INNER), _const_spec((D_INNER, D))],
        out_specs=_row_spec(tile, D),
        out_shape=jax.ShapeDtypeStruct((B, S, D), F32),
        compiler_params=_params(("arbitrary", "arbitrary")),
        name="mla_out_proj",
    )(x, a, w_out)


def _mla_layer(x, positions, g, w_in, q_norm, w_q_up, kv_norm, w_kv_up, w_out):
    w_in_pieces, q_pieces = _mla_column_pieces()
    q, k, v, z = _mla_proj(x, positions, g, _take_columns(w_in, w_in_pieces), q_norm,
                           _take_columns(w_q_up, q_pieces), kv_norm, w_kv_up.astype(BF16))
    gated = _attention(q, k, v, z)
    return _out_proj(x, gated, w_out.astype(BF16))


def kernel(x, positions, pool_norm, pool_w_in, pool_w_grp, pool_scale, pool_w_out,
           conv_norm, conv_w_in, conv_w, conv_w_out,
           mla_norm, mla_w_in, mla_q_norm, mla_w_q_up, mla_kv_norm, mla_w_kv_up, mla_w_out,
           final_norm):
    for i in range(DEPTH):
        kind, j = i % N_MIXERS, i // N_MIXERS
        last = i == DEPTH - 1
        if kind == 0:
            x = _pool_layer(x, pool_norm[j], pool_w_in[j].astype(BF16),
                            pool_w_grp[j].astype(BF16), pool_scale[j],
                            pool_w_out[j].astype(BF16), final_norm if last else None)
        elif kind == 1:
            x = _conv_layer(x, conv_norm[j], conv_w_in[j].astype(BF16), conv_w[j],
                            conv_w_out[j].astype(BF16))
        else:
            x = _mla_layer(x, positions, mla_norm[j], mla_w_in[j], mla_q_norm[j],
                           mla_w_q_up[j], mla_kv_norm[j], mla_w_kv_up[j], mla_w_out[j])
        if last and kind != 0:
            raise NotImplementedError("final norm is fused into a pool layer")
    return x
```

```python
import functools
import math

import numpy as np
import jax
import jax.numpy as jnp
from jax import lax
from jax.experimental import pallas as pl
from jax.experimental.pallas import tpu as pltpu

D_MODEL = 1024
DEPTH = 4
N_MIXERS = 3
D_INNER = 2 * D_MODEL
POOL_WINDOWS = (2, 4, 8, 16)
N_POOL_GROUPS = len(POOL_WINDOWS)
POOL_GROUP = D_INNER // N_POOL_GROUPS
CONV_WIDTH = 3
N_HEADS = 16
N_PAIRS = N_HEADS // 2
QK_NOPE_DIM = 128
QK_ROPE_DIM = 64
ROPE_HALF = QK_ROPE_DIM // 2
V_HEAD_DIM = D_INNER // N_HEADS
Q_LORA_RANK = 384
KV_LORA_RANK = 256
Q_HEAD_DIM = QK_NOPE_DIM + QK_ROPE_DIM
ATTN_SCALE = Q_HEAD_DIM ** -0.5
ROPE_BASE = 10000.0
NORM_EPS = 1e-6
MASK_VALUE = -1e30

V7X_LANES = 128
V7X_SUBLANES = 8
V7X_BF16_SUBLANES = 16
V7X_VMEM_LIMIT_BYTES = 56 * 1024 * 1024

POOL_HALO = max(POOL_WINDOWS)
CONV_HALO = V7X_SUBLANES
SEQ_TILE = 512
POOL_TILE = 1024
CONV_TILE = 1024
OUT_PROJ_TILE = 1024
CONV_CHUNK = 512
ATTN_TQ = 512
ATTN_CHUNK = 1024
EXP2_SCALE = ATTN_SCALE * math.log2(math.e)
PAIR_Q = 2 * QK_NOPE_DIM + 2 * QK_ROPE_DIM
HEAD_K = 2 * QK_NOPE_DIM
MLA_H = Q_LORA_RANK + KV_LORA_RANK + V7X_LANES + D_INNER

F32 = jnp.float32
BF16 = jnp.bfloat16


def _rms_scale(x, g):
    ms = jnp.mean(x * x, axis=-1, keepdims=True)
    return x * lax.rsqrt(ms + NORM_EPS) * g


def _silu(z):
    return z * (1.0 / (1.0 + jnp.exp(-z)))


def _dot(a, b):
    return jnp.dot(a, b, preferred_element_type=F32)


def _const_spec(shape):
    zeros = (0,) * len(shape)
    return pl.BlockSpec(shape, lambda *_: zeros, pipeline_mode=pl.Buffered(1))


def _row_spec(tile, width):
    return pl.BlockSpec((None, tile, width), lambda b, s: (b, s, 0))


def _params(semantics):
    return pltpu.CompilerParams(dimension_semantics=semantics,
                                vmem_limit_bytes=V7X_VMEM_LIMIT_BYTES)


def _pool_kernel(*refs, tile, final):
    if final:
        x_ref, g_ref, win_ref, wgrp_ref, scale_ref, wout_ref, fin_ref, o_ref, tail_ref = refs
    else:
        x_ref, g_ref, win_ref, wgrp_ref, scale_ref, wout_ref, o_ref, tail_ref = refs
    s_idx = pl.program_id(1)

    @pl.when(s_idx == 0)
    def _():
        tail_ref[...] = jnp.zeros_like(tail_ref)

    x = x_ref[...]
    xn = _rms_scale(x, g_ref[...]).astype(BF16)
    pos = s_idx * tile + lax.broadcasted_iota(jnp.int32, (tile, 1), 0)
    count = (pos + 1).astype(F32)
    acc = None
    for g, w in enumerate(POOL_WINDOWS):
        lo, hi = g * POOL_GROUP, (g + 1) * POOL_GROUP
        u = _dot(xn, win_ref[:, lo:hi])
        z = _dot(xn, win_ref[:, D_INNER + lo:D_INNER + hi])
        s = jnp.concatenate([tail_ref[:, lo:hi], u], axis=0)
        k = 1
        while k < w:
            s = s + pltpu.roll(s, k, axis=0)
            k *= 2
        s = s[POOL_HALO:, :]
        inv = 1.0 / jnp.minimum(count, float(w))
        pooled = (s * inv - u).astype(BF16)
        mixed = _dot(pooled, wgrp_ref[g]) * scale_ref[:, lo:hi]
        gated = (mixed * _silu(z)).astype(BF16)
        part = _dot(gated, wout_ref[lo:hi, :])
        acc = part if acc is None else acc + part
        tail_ref[:, lo:hi] = u[tile - POOL_HALO:, :]
    y = x + acc
    if final:
        y = _rms_scale(y, fin_ref[...])
    o_ref[...] = y


def _pool_layer(x, g, w_in, w_grp, scale, w_out, final_g=None):
    B, S, D = x.shape
    tile = POOL_TILE
    final = final_g is not None
    in_specs = [
        _row_spec(tile, D),
        _const_spec((1, D)),
        _const_spec((D, 2 * D_INNER)),
        _const_spec((N_POOL_GROUPS, POOL_GROUP, POOL_GROUP)),
        _const_spec((1, D_INNER)),
        _const_spec((D_INNER, D)),
    ]
    args = [x, g.reshape(1, D), w_in, w_grp, scale.reshape(1, D_INNER), w_out]
    if final:
        in_specs.append(_const_spec((1, D)))
        args.append(final_g.reshape(1, D))
    return pl.pallas_call(
        functools.partial(_pool_kernel, tile=tile, final=final),
        grid=(B, S // tile),
        in_specs=in_specs,
        out_specs=_row_spec(tile, D),
        out_shape=jax.ShapeDtypeStruct((B, S, D), F32),
        scratch_shapes=[pltpu.VMEM((POOL_HALO, D_INNER), F32)],
        compiler_params=_params(("arbitrary", "arbitrary")),
        name="pool_layer",
    )(*args)


def _conv_kernel(x_ref, g_ref, win_ref, cw_ref, wout_ref, o_ref, tail_ref, *, tile):
    s_idx = pl.program_id(1)

    @pl.when(s_idx == 0)
    def _():
        tail_ref[...] = jnp.zeros_like(tail_ref)

    x = x_ref[...]
    xn = _rms_scale(x, g_ref[...]).astype(BF16)
    acc = None
    for c in range(D_INNER // CONV_CHUNK):
        lo, hi = c * CONV_CHUNK, (c + 1) * CONV_CHUNK
        bb = _dot(xn, win_ref[:, lo:hi])
        cc = _dot(xn, win_ref[:, D_INNER + lo:D_INNER + hi])
        hh = _dot(xn, win_ref[:, 2 * D_INNER + lo:2 * D_INNER + hi])
        zz = _dot(xn, win_ref[:, 3 * D_INNER + lo:3 * D_INNER + hi])
        ch = cc * hh
        che = jnp.concatenate([tail_ref[:, lo:hi], ch], axis=0)
        cw = cw_ref[:, lo:hi]
        conv = (che * cw[2:3, :] + pltpu.roll(che, 1, axis=0) * cw[1:2, :]
                + pltpu.roll(che, 2, axis=0) * cw[0:1, :])[CONV_HALO:, :]
        y = (bb * conv * _silu(zz)).astype(BF16)
        part = _dot(y, wout_ref[lo:hi, :])
        acc = part if acc is None else acc + part
        tail_ref[:, lo:hi] = ch[tile - CONV_HALO:, :]
    o_ref[...] = x + acc


def _conv_layer(x, g, w_in, conv_w, w_out):
    B, S, D = x.shape
    tile = CONV_TILE
    return pl.pallas_call(
        functools.partial(_conv_kernel, tile=tile),
        grid=(B, S // tile),
        in_specs=[
            _row_spec(tile, D),
            _const_spec((1, D)),
            _const_spec((D, 4 * D_INNER)),
            _const_spec((CONV_WIDTH, D_INNER)),
            _const_spec((D_INNER, D)),
        ],
        out_specs=_row_spec(tile, D),
        out_shape=jax.ShapeDtypeStruct((B, S, D), F32),
        scratch_shapes=[pltpu.VMEM((CONV_HALO, D_INNER), F32)],
        compiler_params=_params(("arbitrary", "arbitrary")),
        name="conv_layer",
    )(x, g.reshape(1, D), w_in, conv_w, w_out)


def _rope_constants():
    inv_freq = ROPE_BASE ** (-np.arange(0, QK_ROPE_DIM, 2, dtype=np.float32) / QK_ROPE_DIM)
    lane_group = np.arange(V7X_LANES) // ROPE_HALF
    rows = np.zeros((V7X_SUBLANES, V7X_LANES), np.float32)
    rows[0] = np.tile(inv_freq.astype(np.float32), V7X_LANES // ROPE_HALF)
    rows[1] = np.where(lane_group < 2, -1.0, 1.0)
    rows[2] = (lane_group % 2 == 0)
    rows[3] = (lane_group % 2 == 1)
    return rows


def _mla_column_pieces():
    kr0 = Q_LORA_RANK + KV_LORA_RANK
    w_in_pieces = [(0, kr0),
                   (kr0, ROPE_HALF), (kr0, ROPE_HALF),
                   (kr0 + ROPE_HALF, ROPE_HALF), (kr0 + ROPE_HALF, ROPE_HALF),
                   (kr0 + QK_ROPE_DIM, D_INNER)]
    q_pieces = []
    for p in range(N_PAIRS):
        h0, h1 = 2 * p * Q_HEAD_DIM, (2 * p + 1) * Q_HEAD_DIM
        q_pieces += [(h0, QK_NOPE_DIM),
                     (h0 + QK_NOPE_DIM, ROPE_HALF), (h1 + QK_NOPE_DIM, ROPE_HALF),
                     (h0 + QK_NOPE_DIM + ROPE_HALF, ROPE_HALF),
                     (h1 + QK_NOPE_DIM + ROPE_HALF, ROPE_HALF),
                     (h1, QK_NOPE_DIM)]
    return w_in_pieces, q_pieces


def _take_columns(w, pieces):
    return jnp.concatenate([w[:, a:a + n] for a, n in pieces], axis=1).astype(BF16)


def _mla_proj_kernel(x_ref, pos_ref, g_ref, win_ref, qn_ref, wq_ref, kvn_ref, wkv_ref, rc_ref,
                     q_ref, k_ref, v_ref, z_ref):
    xn = _rms_scale(x_ref[...], g_ref[...]).astype(BF16)
    kv0 = Q_LORA_RANK
    kr0 = kv0 + KV_LORA_RANK
    z0 = kr0 + V7X_LANES
    q_lat = _dot(xn, win_ref[:, :kv0])
    kv_lat = _dot(xn, win_ref[:, kv0:kr0])
    kr = _dot(xn, win_ref[:, kr0:z0])
    z_ref[...] = _dot(xn, win_ref[:, z0:])

    angles = pos_ref[...].astype(F32) * rc_ref[0:1, :]
    cos = jnp.cos(angles)
    sin = jnp.sin(angles) * rc_ref[1:2, :]

    def rope(t):
        return t * cos + pltpu.roll(t, 2 * ROPE_HALF, axis=1) * sin

    qn = _rms_scale(q_lat, qn_ref[...]).astype(BF16)
    for p in range(N_PAIRS):
        qp = _dot(qn, wq_ref[:, p * PAIR_Q:(p + 1) * PAIR_Q]) * EXP2_SCALE
        q_rope = rope(qp[:, QK_NOPE_DIM:2 * QK_NOPE_DIM]).astype(BF16)
        lo = 2 * p * HEAD_K
        q_ref[:, lo:lo + QK_NOPE_DIM] = qp[:, :QK_NOPE_DIM].astype(BF16)
        q_ref[:, lo + QK_NOPE_DIM:lo + HEAD_K] = q_rope
        q_ref[:, lo + HEAD_K:lo + HEAD_K + QK_NOPE_DIM] = q_rope
        q_ref[:, lo + HEAD_K + QK_NOPE_DIM:lo + 2 * HEAD_K] = qp[:, 2 * QK_NOPE_DIM:].astype(BF16)

    kr = rope(kr)
    kr_even = (kr * rc_ref[2:3, :]).astype(BF16)
    kr_odd = (kr * rc_ref[3:4, :]).astype(BF16)
    kvn = _rms_scale(kv_lat, kvn_ref[...]).astype(BF16)
    for h in range(N_HEADS):
        lo = h * HEAD_K
        kv = _dot(kvn, wkv_ref[:, lo:lo + HEAD_K])
        k_nope = kv[:, :QK_NOPE_DIM].astype(BF16)
        v_ref[:, h * V_HEAD_DIM:(h + 1) * V_HEAD_DIM] = kv[:, QK_NOPE_DIM:].astype(BF16)
        if h % 2 == 0:
            k_ref[:, lo:lo + QK_NOPE_DIM] = k_nope
            k_ref[:, lo + QK_NOPE_DIM:lo + HEAD_K] = kr_even
        else:
            k_ref[:, lo:lo + QK_NOPE_DIM] = kr_odd
            k_ref[:, lo + QK_NOPE_DIM:lo + HEAD_K] = k_nope


def _mla_proj(x, positions, g, w_in, q_norm, w_q_up, kv_norm, w_kv_up):
    B, S, D = x.shape
    tile = SEQ_TILE
    n_q, n_k = N_HEADS * HEAD_K, N_HEADS * HEAD_K
    return pl.pallas_call(
        _mla_proj_kernel,
        grid=(B, S // tile),
        in_specs=[
            _row_spec(tile, D),
            _row_spec(tile, 1),
            _const_spec((1, D)),
            _const_spec((D, MLA_H)),
            _const_spec((1, Q_LORA_RANK)),
            _const_spec((Q_LORA_RANK, N_PAIRS * PAIR_Q)),
            _const_spec((1, KV_LORA_RANK)),
            _const_spec((KV_LORA_RANK, n_k)),
            _const_spec((V7X_SUBLANES, V7X_LANES)),
        ],
        out_specs=[_row_spec(tile, n_q), _row_spec(tile, n_k),
                   _row_spec(tile, D_INNER), _row_spec(tile, D_INNER)],
        out_shape=[jax.ShapeDtypeStruct((B, S, n_q), BF16),
                   jax.ShapeDtypeStruct((B, S, n_k), BF16),
                   jax.ShapeDtypeStruct((B, S, D_INNER), BF16),
                   jax.ShapeDtypeStruct((B, S, D_INNER), F32)],
        compiler_params=_params(("arbitrary", "arbitrary")),
        name="mla_proj",
    )(x, positions.reshape(B, S, 1), g.reshape(1, D), w_in, q_norm.reshape(1, -1), w_q_up,
      kv_norm.reshape(1, -1), w_kv_up, jnp.asarray(_rope_constants()))


def _attn_kernel(q_ref, k_ref, v_ref, z_ref, o_ref, s_ref, vxt_ref, *, seq):
    nt = (((1,), (1,)), ((), ()))
    n_tiles = seq // ATTN_TQ
    half = ATTN_TQ // 2
    traced_zero = jnp.minimum(pl.program_id(1), 0)
    vxt_ref[:V_HEAD_DIM, :] = v_ref[...].astype(F32).T.astype(BF16)
    pad_row = lax.broadcasted_iota(jnp.int32, (V7X_BF16_SUBLANES, seq), 0)
    vxt_ref[V_HEAD_DIM:, :] = jnp.where(pad_row == 0, 1.0, 0.0).astype(BF16)

    def causal(s, first_key, first_query):
        key = first_key + lax.broadcasted_iota(jnp.int32, s.shape, 0)
        query = first_query + lax.broadcasted_iota(jnp.int32, s.shape, 1)
        return jnp.where(key <= query, s, MASK_VALUE)

    def col_max(m, s):
        top = jnp.max(s, axis=0, keepdims=True)
        return top if m is None else jnp.maximum(m, top)

    def scores(i):
        buf, lo, hi = traced_zero + i % 2, i * ATTN_TQ, (i + 1) * ATTN_TQ
        mid = lo + half
        q = q_ref[lo:hi, :]
        m = None
        for c0 in range(0, lo, ATTN_CHUNK):
            c1 = min(c0 + ATTN_CHUNK, lo)
            s = lax.dot_general(k_ref[c0:c1, :], q, nt, preferred_element_type=F32)
            s_ref[buf, c0:c1, :] = s
            m = col_max(m, s)
        s = causal(lax.dot_general(k_ref[lo:mid, :], q, nt, preferred_element_type=F32), lo, lo)
        s_ref[buf, lo:mid, :] = s
        m = col_max(m, s)
        s = causal(lax.dot_general(k_ref[mid:hi, :], q[half:, :], nt,
                                   preferred_element_type=F32), mid, mid)
        s_ref[buf, mid:hi, half:] = s
        m = jnp.concatenate([m[:, :half], col_max(m[:, half:], s)], axis=1)
        return m

    def weights(i, mc):
        buf, lo, hi = traced_zero + i % 2, i * ATTN_TQ, (i + 1) * ATTN_TQ
        mid = lo + half

        def probs(r0, r1, c0, c1):
            return jnp.exp2(s_ref[buf, r0:r1, c0:c1] - mc[:, c0:c1]).astype(BF16)

        acc = None
        for c0 in range(0, mid, ATTN_CHUNK):
            c1 = min(c0 + ATTN_CHUNK, mid)
            part = _dot(vxt_ref[:, c0:c1], probs(c0, c1, 0, ATTN_TQ))
            acc = part if acc is None else acc + part
        late = _dot(vxt_ref[:, mid:hi], probs(mid, hi, half, ATTN_TQ))
        acc = jnp.concatenate([acc[:, :half], acc[:, half:] + late], axis=1)
        o_t = acc[:V_HEAD_DIM, :] / acc[V_HEAD_DIM:V_HEAD_DIM + 1, :]
        o_ref[lo:hi, :] = (o_t.T * _silu(z_ref[lo:hi, :])).astype(BF16)

    mc = scores(0)
    for i in range(n_tiles):
        mc_next = scores(i + 1) if i + 1 < n_tiles else None
        weights(i, mc)
        mc = mc_next


def _attention(q, k, v, z):
    B, S, _ = q.shape

    def spec(width):
        return pl.BlockSpec((None, S, width), lambda b, h: (b, 0, h))

    return pl.pallas_call(
        functools.partial(_attn_kernel, seq=S),
        grid=(B, N_HEADS),
        in_specs=[spec(HEAD_K), spec(HEAD_K), spec(V_HEAD_DIM), spec(V_HEAD_DIM)],
        out_specs=spec(V_HEAD_DIM),
        out_shape=jax.ShapeDtypeStruct((B, S, D_INNER), BF16),
        scratch_shapes=[
            pltpu.VMEM((2, S, ATTN_TQ), F32),
            pltpu.VMEM((V_HEAD_DIM + V7X_BF16_SUBLANES, S), BF16),
        ],
        compiler_params=_params(("arbitrary", "arbitrary")),
        name="mla_attention",
    )(q, k, v, z)


def _out_proj_kernel(x_ref, a_ref, w_ref, o_ref):
    o_ref[...] = x_ref[...] + _dot(a_ref[...], w_ref[...])


def _out_proj(x, a, w_out):
    B, S, D = x.shape
    tile = OUT_PROJ_TILE
    return pl.pallas_call(
        _out_proj_kernel,
        grid=(B, S // tile),
        in_specs=[_row_spec(tile, D), _row_spec(tile, D_INNER), _const_spec((D_INNER, D))],
        out_specs=_row_spec(tile, D),
        out_shape=jax.ShapeDtypeStruct((B, S, D), F32),
        compiler_params=_params(("arbitrary", "arbitrary")),
        name="mla_out_proj",
    )(x, a, w_out)


def _mla_layer(x, positions, g, w_in, q_norm, w_q_up, kv_norm, w_kv_up, w_out):
    w_in_pieces, q_pieces = _mla_column_pieces()
    q, k, v, z = _mla_proj(x, positions, g, _take_columns(w_in, w_in_pieces), q_norm,
                           _take_columns(w_q_up, q_pieces), kv_norm, w_kv_up.astype(BF16))
    gated = _attention(q, k, v, z)
    return _out_proj(x, gated, w_out.astype(BF16))


def kernel(x, positions, pool_norm, pool_w_in, pool_w_grp, pool_scale, pool_w_out,
           conv_norm, conv_w_in, conv_w, conv_w_out,
           mla_norm, mla_w_in, mla_q_norm, mla_w_q_up, mla_kv_norm, mla_w_kv_up, mla_w_out,
           final_norm):
    for i in range(DEPTH):
        kind, j = i % N_MIXERS, i // N_MIXERS
        last = i == DEPTH - 1
        if kind == 0:
            x = _pool_layer(x, pool_norm[j], pool_w_in[j].astype(BF16),
                            pool_w_grp[j].astype(BF16), pool_scale[j],
                            pool_w_out[j].astype(BF16), final_norm if last else None)
        elif kind == 1:
            x = _conv_layer(x, conv_norm[j], conv_w_in[j].astype(BF16), conv_w[j],
                            conv_w_out[j].astype(BF16))
        else:
            x = _mla_layer(x, positions, mla_norm[j], mla_w_in[j], mla_q_norm[j],
                           mla_w_q_up[j], mla_kv_norm[j], mla_w_kv_up[j], mla_w_out[j])
        if last and kind != 0:
            raise NotImplementedError("final norm is fused into a pool layer")
    return x
```

```python
import functools
import math

import numpy as np
import jax
import jax.numpy as jnp
from jax import lax
from jax.experimental import pallas as pl
from jax.experimental.pallas import tpu as pltpu

D_MODEL = 1024
DEPTH = 4
N_MIXERS = 3
D_INNER = 2 * D_MODEL
POOL_WINDOWS = (2, 4, 8, 16)
N_POOL_GROUPS = len(POOL_WINDOWS)
POOL_GROUP = D_INNER // N_POOL_GROUPS
CONV_WIDTH = 3
N_HEADS = 16
N_PAIRS = N_HEADS // 2
QK_NOPE_DIM = 128
QK_ROPE_DIM = 64
ROPE_HALF = QK_ROPE_DIM // 2
V_HEAD_DIM = D_INNER // N_HEADS
Q_LORA_RANK = 384
KV_LORA_RANK = 256
Q_HEAD_DIM = QK_NOPE_DIM + QK_ROPE_DIM
ATTN_SCALE = Q_HEAD_DIM ** -0.5
ROPE_BASE = 10000.0
NORM_EPS = 1e-6
MASK_VALUE = -1e30

V7X_LANES = 128
V7X_SUBLANES = 8
V7X_BF16_SUBLANES = 16
V7X_VMEM_LIMIT_BYTES = 56 * 1024 * 1024

POOL_HALO = max(POOL_WINDOWS)
CONV_HALO = V7X_SUBLANES
SEQ_TILE = 512
POOL_TILE = 1024
CONV_TILE = 1024
OUT_PROJ_TILE = 1024
CONV_CHUNK = 512
ATTN_TQ = 512
ATTN_CHUNK = 512
EXP2_SCALE = ATTN_SCALE * math.log2(math.e)
PAIR_Q = 2 * QK_NOPE_DIM + 2 * QK_ROPE_DIM
HEAD_K = 2 * QK_NOPE_DIM
MLA_H = Q_LORA_RANK + KV_LORA_RANK + V7X_LANES + D_INNER

F32 = jnp.float32
BF16 = jnp.bfloat16


def _rms_scale(x, g):
    ms = jnp.mean(x * x, axis=-1, keepdims=True)
    return x * lax.rsqrt(ms + NORM_EPS) * g


def _silu(z):
    return z * (1.0 / (1.0 + jnp.exp(-z)))


def _dot(a, b):
    return jnp.dot(a, b, preferred_element_type=F32)


def _const_spec(shape):
    zeros = (0,) * len(shape)
    return pl.BlockSpec(shape, lambda *_: zeros, pipeline_mode=pl.Buffered(1))


def _row_spec(tile, width):
    return pl.BlockSpec((None, tile, width), lambda b, s: (b, s, 0))


def _params(semantics):
    return pltpu.CompilerParams(dimension_semantics=semantics,
                                vmem_limit_bytes=V7X_VMEM_LIMIT_BYTES)


def _pool_kernel(*refs, tile, final):
    if final:
        x_ref, g_ref, win_ref, wgrp_ref, scale_ref, wout_ref, fin_ref, o_ref, tail_ref = refs
    else:
        x_ref, g_ref, win_ref, wgrp_ref, scale_ref, wout_ref, o_ref, tail_ref = refs
    s_idx = pl.program_id(1)

    @pl.when(s_idx == 0)
    def _():
        tail_ref[...] = jnp.zeros_like(tail_ref)

    x = x_ref[...]
    xn = _rms_scale(x, g_ref[...]).astype(BF16)
    pos = s_idx * tile + lax.broadcasted_iota(jnp.int32, (tile, 1), 0)
    count = (pos + 1).astype(F32)
    acc = None
    for g, w in enumerate(POOL_WINDOWS):
        lo, hi = g * POOL_GROUP, (g + 1) * POOL_GROUP
        u = _dot(xn, win_ref[:, lo:hi])
        z = _dot(xn, win_ref[:, D_INNER + lo:D_INNER + hi])
        s = jnp.concatenate([tail_ref[:, lo:hi], u], axis=0)
        k = 1
        while k < w:
            s = s + pltpu.roll(s, k, axis=0)
            k *= 2
        s = s[POOL_HALO:, :]
        inv = 1.0 / jnp.minimum(count, float(w))
        pooled = (s * inv - u).astype(BF16)
        mixed = _dot(pooled, wgrp_ref[g]) * scale_ref[:, lo:hi]
        gated = (mixed * _silu(z)).astype(BF16)
        part = _dot(gated, wout_ref[lo:hi, :])
        acc = part if acc is None else acc + part
        tail_ref[:, lo:hi] = u[tile - POOL_HALO:, :]
    y = x + acc
    if final:
        y = _rms_scale(y, fin_ref[...])
    o_ref[...] = y


def _pool_layer(x, g, w_in, w_grp, scale, w_out, final_g=None):
    B, S, D = x.shape
    tile = POOL_TILE
    final = final_g is not None
    in_specs = [
        _row_spec(tile, D),
        _const_spec((1, D)),
        _const_spec((D, 2 * D_INNER)),
        _const_spec((N_POOL_GROUPS, POOL_GROUP, POOL_GROUP)),
        _const_spec((1, D_INNER)),
        _const_spec((D_INNER, D)),
    ]
    args = [x, g.reshape(1, D), w_in, w_grp, scale.reshape(1, D_INNER), w_out]
    if final:
        in_specs.append(_const_spec((1, D)))
        args.append(final_g.reshape(1, D))
    return pl.pallas_call(
        functools.partial(_pool_kernel, tile=tile, final=final),
        grid=(B, S // tile),
        in_specs=in_specs,
        out_specs=_row_spec(tile, D),
        out_shape=jax.ShapeDtypeStruct((B, S, D), F32),
        scratch_shapes=[pltpu.VMEM((POOL_HALO, D_INNER), F32)],
        compiler_params=_params(("arbitrary", "arbitrary")),
        name="pool_layer",
    )(*args)


def _conv_kernel(x_ref, g_ref, win_ref, cw_ref, wout_ref, o_ref, tail_ref, *, tile):
    s_idx = pl.program_id(1)

    @pl.when(s_idx == 0)
    def _():
        tail_ref[...] = jnp.zeros_like(tail_ref)

    x = x_ref[...]
    xn = _rms_scale(x, g_ref[...]).astype(BF16)
    acc = None
    for c in range(D_INNER // CONV_CHUNK):
        lo, hi = c * CONV_CHUNK, (c + 1) * CONV_CHUNK
        bb = _dot(xn, win_ref[:, lo:hi])
        cc = _dot(xn, win_ref[:, D_INNER + lo:D_INNER + hi])
        hh = _dot(xn, win_ref[:, 2 * D_INNER + lo:2 * D_INNER + hi])
        zz = _dot(xn, win_ref[:, 3 * D_INNER + lo:3 * D_INNER + hi])
        ch = cc * hh
        che = jnp.concatenate([tail_ref[:, lo:hi], ch], axis=0)
        cw = cw_ref[:, lo:hi]
        conv = (che * cw[2:3, :] + pltpu.roll(che, 1, axis=0) * cw[1:2, :]
                + pltpu.roll(che, 2, axis=0) * cw[0:1, :])[CONV_HALO:, :]
        y = (bb * conv * _silu(zz)).astype(BF16)
        part = _dot(y, wout_ref[lo:hi, :])
        acc = part if acc is None else acc + part
        tail_ref[:, lo:hi] = ch[tile - CONV_HALO:, :]
    o_ref[...] = x + acc


def _conv_layer(x, g, w_in, conv_w, w_out):
    B, S, D = x.shape
    tile = CONV_TILE
    return pl.pallas_call(
        functools.partial(_conv_kernel, tile=tile),
        grid=(B, S // tile),
        in_specs=[
            _row_spec(tile, D),
            _const_spec((1, D)),
            _const_spec((D, 4 * D_INNER)),
            _const_spec((CONV_WIDTH, D_INNER)),
            _const_spec((D_INNER, D)),
        ],
        out_specs=_row_spec(tile, D),
        out_shape=jax.ShapeDtypeStruct((B, S, D), F32),
        scratch_shapes=[pltpu.VMEM((CONV_HALO, D_INNER), F32)],
        compiler_params=_params(("arbitrary", "arbitrary")),
        name="conv_layer",
    )(x, g.reshape(1, D), w_in, conv_w, w_out)


def _rope_constants():
    inv_freq = ROPE_BASE ** (-np.arange(0, QK_ROPE_DIM, 2, dtype=np.float32) / QK_ROPE_DIM)
    lane_group = np.arange(V7X_LANES) // ROPE_HALF
    rows = np.zeros((V7X_SUBLANES, V7X_LANES), np.float32)
    rows[0] = np.tile(inv_freq.astype(np.float32), V7X_LANES // ROPE_HALF)
    rows[1] = np.where(lane_group < 2, -1.0, 1.0)
    rows[2] = (lane_group % 2 == 0)
    rows[3] = (lane_group % 2 == 1)
    return rows


def _mla_column_pieces():
    kr0 = Q_LORA_RANK + KV_LORA_RANK
    w_in_pieces = [(0, kr0),
                   (kr0, ROPE_HALF), (kr0, ROPE_HALF),
                   (kr0 + ROPE_HALF, ROPE_HALF), (kr0 + ROPE_HALF, ROPE_HALF),
                   (kr0 + QK_ROPE_DIM, D_INNER)]
    q_pieces = []
    for p in range(N_PAIRS):
        h0, h1 = 2 * p * Q_HEAD_DIM, (2 * p + 1) * Q_HEAD_DIM
        q_pieces += [(h0, QK_NOPE_DIM),
                     (h0 + QK_NOPE_DIM, ROPE_HALF), (h1 + QK_NOPE_DIM, ROPE_HALF),
                     (h0 + QK_NOPE_DIM + ROPE_HALF, ROPE_HALF),
                     (h1 + QK_NOPE_DIM + ROPE_HALF, ROPE_HALF),
                     (h1, QK_NOPE_DIM)]
    return w_in_pieces, q_pieces


def _take_columns(w, pieces):
    return jnp.concatenate([w[:, a:a + n] for a, n in pieces], axis=1).astype(BF16)


def _mla_proj_kernel(x_ref, pos_ref, g_ref, win_ref, qn_ref, wq_ref, kvn_ref, wkv_ref, rc_ref,
                     q_ref, k_ref, v_ref, z_ref):
    xn = _rms_scale(x_ref[...], g_ref[...]).astype(BF16)
    kv0 = Q_LORA_RANK
    kr0 = kv0 + KV_LORA_RANK
    z0 = kr0 + V7X_LANES
    q_lat = _dot(xn, win_ref[:, :kv0])
    kv_lat = _dot(xn, win_ref[:, kv0:kr0])
    kr = _dot(xn, win_ref[:, kr0:z0])
    z = _dot(xn, win_ref[:, z0:])
    for h in range(N_HEADS):
        z_ref[h] = z[:, h * V_HEAD_DIM:(h + 1) * V_HEAD_DIM]

    angles = pos_ref[...].astype(F32) * rc_ref[0:1, :]
    cos = jnp.cos(angles)
    sin = jnp.sin(angles) * rc_ref[1:2, :]

    def rope(t):
        return t * cos + pltpu.roll(t, 2 * ROPE_HALF, axis=1) * sin

    qn = _rms_scale(q_lat, qn_ref[...]).astype(BF16)
    for p in range(N_PAIRS):
        qp = _dot(qn, wq_ref[:, p * PAIR_Q:(p + 1) * PAIR_Q]) * EXP2_SCALE
        q_rope = rope(qp[:, QK_NOPE_DIM:2 * QK_NOPE_DIM]).astype(BF16)
        q_ref[2 * p, :, :QK_NOPE_DIM] = qp[:, :QK_NOPE_DIM].astype(BF16)
        q_ref[2 * p, :, QK_NOPE_DIM:] = q_rope
        q_ref[2 * p + 1, :, :QK_NOPE_DIM] = q_rope
        q_ref[2 * p + 1, :, QK_NOPE_DIM:] = qp[:, 2 * QK_NOPE_DIM:].astype(BF16)

    kr = rope(kr)
    kr_even = (kr * rc_ref[2:3, :]).astype(BF16)
    kr_odd = (kr * rc_ref[3:4, :]).astype(BF16)
    kvn = _rms_scale(kv_lat, kvn_ref[...]).astype(BF16)
    for h in range(N_HEADS):
        lo = h * HEAD_K
        kv = _dot(kvn, wkv_ref[:, lo:lo + HEAD_K])
        k_nope = kv[:, :QK_NOPE_DIM].astype(BF16)
        v_ref[h] = kv[:, QK_NOPE_DIM:].astype(BF16)
        if h % 2 == 0:
            k_ref[h, :, :QK_NOPE_DIM] = k_nope
            k_ref[h, :, QK_NOPE_DIM:] = kr_even
        else:
            k_ref[h, :, :QK_NOPE_DIM] = kr_odd
            k_ref[h, :, QK_NOPE_DIM:] = k_nope


def _mla_proj(x, positions, g, w_in, q_norm, w_q_up, kv_norm, w_kv_up):
    B, S, D = x.shape
    tile = SEQ_TILE

    def head_spec(width):
        return pl.BlockSpec((None, N_HEADS, tile, width), lambda b, s: (b, 0, s, 0))

    def head_shape(width, dtype):
        return jax.ShapeDtypeStruct((B, N_HEADS, S, width), dtype)

    return pl.pallas_call(
        _mla_proj_kernel,
        grid=(B, S // tile),
        in_specs=[
            _row_spec(tile, D),
            _row_spec(tile, 1),
            _const_spec((1, D)),
            _const_spec((D, MLA_H)),
            _const_spec((1, Q_LORA_RANK)),
            _const_spec((Q_LORA_RANK, N_PAIRS * PAIR_Q)),
            _const_spec((1, KV_LORA_RANK)),
            _const_spec((KV_LORA_RANK, N_HEADS * HEAD_K)),
            _const_spec((V7X_SUBLANES, V7X_LANES)),
        ],
        out_specs=[head_spec(HEAD_K), head_spec(HEAD_K),
                   head_spec(V_HEAD_DIM), head_spec(V_HEAD_DIM)],
        out_shape=[head_shape(HEAD_K, BF16), head_shape(HEAD_K, BF16),
                   head_shape(V_HEAD_DIM, BF16), head_shape(V_HEAD_DIM, F32)],
        compiler_params=_params(("arbitrary", "arbitrary")),
        name="mla_proj",
    )(x, positions.reshape(B, S, 1), g.reshape(1, D), w_in, q_norm.reshape(1, -1), w_q_up,
      kv_norm.reshape(1, -1), w_kv_up, jnp.asarray(_rope_constants()))


def _attn_kernel(q_ref, k_ref, v_ref, z_ref, o_ref, s_even_ref, s_odd_ref, vxt_ref, *, seq):
    nt = (((1,), (1,)), ((), ()))
    n_tiles = seq // ATTN_TQ
    half = ATTN_TQ // 2
    traced_zero = jnp.minimum(pl.program_id(1), 0)
    s_refs = (s_even_ref, s_odd_ref)
    vxt_ref[:V_HEAD_DIM, :] = v_ref[...].astype(F32).T.astype(BF16)
    pad_row = lax.broadcasted_iota(jnp.int32, (V7X_BF16_SUBLANES, seq), 0)
    vxt_ref[V_HEAD_DIM:, :] = jnp.where(pad_row == 0, 1.0, 0.0).astype(BF16)

    def causal(s, first_key, first_query):
        key = first_key + lax.broadcasted_iota(jnp.int32, s.shape, 0)
        query = first_query + lax.broadcasted_iota(jnp.int32, s.shape, 1)
        return jnp.where(key <= query, s, MASK_VALUE)

    def col_max(m, s):
        top = jnp.max(s, axis=0, keepdims=True)
        return top if m is None else jnp.maximum(m, top)

    def scores(i, result):
        s_ref, buf, lo, hi = s_refs[i % 2], traced_zero, i * ATTN_TQ, (i + 1) * ATTN_TQ
        mid = lo + half
        q = q_ref[lo:hi, :]
        m = None
        for c0 in range(0, lo, ATTN_CHUNK):
            c1 = min(c0 + ATTN_CHUNK, lo)
            s = lax.dot_general(k_ref[c0:c1, :], q, nt, preferred_element_type=F32)
            s_ref[buf, c0:c1, :] = s
            m = col_max(m, s)
            yield
        s = causal(lax.dot_general(k_ref[lo:mid, :], q, nt, preferred_element_type=F32), lo, lo)
        s_ref[buf, lo:mid, :] = s
        m = col_max(m, s)
        s = causal(lax.dot_general(k_ref[mid:hi, :], q[half:, :], nt,
                                   preferred_element_type=F32), mid, mid)
        s_ref[buf, mid:hi, half:] = s
        result.append(jnp.concatenate([m[:, :half], col_max(m[:, half:], s)], axis=1))

    def weights(i, mc):
        s_ref, buf, lo, hi = s_refs[i % 2], traced_zero, i * ATTN_TQ, (i + 1) * ATTN_TQ
        mid = lo + half

        def probs(r0, r1, c0, c1):
            return jnp.exp2(s_ref[buf, r0:r1, c0:c1] - mc[:, c0:c1]).astype(BF16)

        acc = None
        for c0 in range(0, mid, ATTN_CHUNK):
            c1 = min(c0 + ATTN_CHUNK, mid)
            part = _dot(vxt_ref[:, c0:c1], probs(c0, c1, 0, ATTN_TQ))
            acc = part if acc is None else acc + part
            yield
        late = _dot(vxt_ref[:, mid:hi], probs(mid, hi, half, ATTN_TQ))
        acc = jnp.concatenate([acc[:, :half], acc[:, half:] + late], axis=1)
        o_t = acc[:V_HEAD_DIM, :] / acc[V_HEAD_DIM:V_HEAD_DIM + 1, :]
        o_ref[lo:hi, :] = (o_t.T * _silu(z_ref[lo:hi, :])).astype(BF16)

    def emit_interleaved(*pieces):
        live = list(pieces)
        while live:
            for piece in list(live):
                if next(piece, "done") == "done":
                    live.remove(piece)

    order = list(range(n_tiles - 1, -1, -1))
    result = []
    emit_interleaved(scores(order[0], result))
    for n, i in enumerate(order):
        mc = result.pop()
        if n + 1 < n_tiles:
            emit_interleaved(scores(order[n + 1], result), weights(i, mc))
        else:
            emit_interleaved(weights(i, mc))


def _attention(q, k, v, z):
    B, _, S, _ = q.shape

    def head_spec(width):
        return pl.BlockSpec((None, None, S, width), lambda b, h: (b, h, 0, 0))

    return pl.pallas_call(
        functools.partial(_attn_kernel, seq=S),
        grid=(B, N_HEADS),
        in_specs=[head_spec(HEAD_K), head_spec(HEAD_K),
                  head_spec(V_HEAD_DIM), head_spec(V_HEAD_DIM)],
        out_specs=pl.BlockSpec((None, S, V_HEAD_DIM), lambda b, h: (b, 0, h)),
        out_shape=jax.ShapeDtypeStruct((B, S, D_INNER), BF16),
        scratch_shapes=[
            pltpu.VMEM((1, S, ATTN_TQ), F32),
            pltpu.VMEM((1, S, ATTN_TQ), F32),
            pltpu.VMEM((V_HEAD_DIM + V7X_BF16_SUBLANES, S), BF16),
        ],
        compiler_params=_params(("arbitrary", "arbitrary")),
        name="mla_attention",
    )(q, k, v, z)


def _out_proj_kernel(x_ref, a_ref, w_ref, o_ref):
    o_ref[...] = x_ref[...] + _dot(a_ref[...], w_ref[...])


def _out_proj(x, a, w_out):
    B, S, D = x.shape
    tile = OUT_PROJ_TILE
    return pl.pallas_call(
        _out_proj_kernel,
        grid=(B, S // tile),
        in_specs=[_row_spec(tile, D), _row_spec(tile, D_INNER), _const_spec((D_INNER, D))],
        out_specs=_row_spec(tile, D),
        out_shape=jax.ShapeDtypeStruct((B, S, D), F32),
        compiler_params=_params(("arbitrary", "arbitrary")),
        name="mla_out_proj",
    )(x, a, w_out)


def _mla_layer(x, positions, g, w_in, q_norm, w_q_up, kv_norm, w_kv_up, w_out):
    w_in_pieces, q_pieces = _mla_column_pieces()
    q, k, v, z = _mla_proj(x, positions, g, _take_columns(w_in, w_in_pieces), q_norm,
                           _take_columns(w_q_up, q_pieces), kv_norm, w_kv_up.astype(BF16))
    gated = _attention(q, k, v, z)
    return _out_proj(x, gated, w_out.astype(BF16))


def kernel(x, positions, pool_norm, pool_w_in, pool_w_grp, pool_scale, pool_w_out,
           conv_norm, conv_w_in, conv_w, conv_w_out,
           mla_norm, mla_w_in, mla_q_norm, mla_w_q_up, mla_kv_norm, mla_w_kv_up, mla_w_out,
           final_norm):
    for i in range(DEPTH):
        kind, j = i % N_MIXERS, i // N_MIXERS
        last = i == DEPTH - 1
        if kind == 0:
            x = _pool_layer(x, pool_norm[j], pool_w_in[j].astype(BF16),
                            pool_w_grp[j].astype(BF16), pool_scale[j],
                            pool_w_out[j].astype(BF16), final_norm if last else None)
        elif kind == 1:
            x = _conv_layer(x, conv_norm[j], conv_w_in[j].astype(BF16), conv_w[j],
                            conv_w_out[j].astype(BF16))
        else:
            x = _mla_layer(x, positions, mla_norm[j], mla_w_in[j], mla_q_norm[j],
                           mla_w_q_up[j], mla_kv_norm[j], mla_w_kv_up[j], mla_w_out[j])
        if last and kind != 0:
            raise NotImplementedError("final norm is fused into a pool layer")
    return x
```

```python
import functools
import math

import numpy as np
import jax
import jax.numpy as jnp
from jax import lax
from jax.experimental import pallas as pl
from jax.experimental.pallas import tpu as pltpu

D_MODEL = 1024
DEPTH = 4
N_MIXERS = 3
D_INNER = 2 * D_MODEL
POOL_WINDOWS = (2, 4, 8, 16)
N_POOL_GROUPS = len(POOL_WINDOWS)
POOL_GROUP = D_INNER // N_POOL_GROUPS
CONV_WIDTH = 3
N_HEADS = 16
N_PAIRS = N_HEADS // 2
QK_NOPE_DIM = 128
QK_ROPE_DIM = 64
ROPE_HALF = QK_ROPE_DIM // 2
V_HEAD_DIM = D_INNER // N_HEADS
Q_LORA_RANK = 384
KV_LORA_RANK = 256
Q_HEAD_DIM = QK_NOPE_DIM + QK_ROPE_DIM
ATTN_SCALE = Q_HEAD_DIM ** -0.5
ROPE_BASE = 10000.0
NORM_EPS = 1e-6
MASK_VALUE = -1e30

V7X_LANES = 128
V7X_SUBLANES = 8
V7X_BF16_SUBLANES = 16
V7X_VMEM_LIMIT_BYTES = 56 * 1024 * 1024

POOL_HALO = max(POOL_WINDOWS)
CONV_HALO = V7X_SUBLANES
SEQ_TILE = 512
PROJ_PIECES = 4
POOL_TILE = 1024
CONV_TILE = 1024
OUT_PROJ_TILE = 1024
CONV_CHUNK = 512
ATTN_TQ = 512
ATTN_CHUNK = 512
EXP2_SCALE = ATTN_SCALE * math.log2(math.e)
PAIR_Q = 2 * QK_NOPE_DIM + 2 * QK_ROPE_DIM
HEAD_K = 2 * QK_NOPE_DIM
MLA_H = Q_LORA_RANK + KV_LORA_RANK + V7X_LANES + D_INNER

F32 = jnp.float32
BF16 = jnp.bfloat16


def _rms_scale(x, g):
    ms = jnp.mean(x * x, axis=-1, keepdims=True)
    return x * lax.rsqrt(ms + NORM_EPS) * g


def _silu(z):
    return z * (1.0 / (1.0 + jnp.exp(-z)))


def _dot(a, b):
    return jnp.dot(a, b, preferred_element_type=F32)


def _const_spec(shape):
    zeros = (0,) * len(shape)
    return pl.BlockSpec(shape, lambda *_: zeros, pipeline_mode=pl.Buffered(1))


def _layer_spec(shape, j):
    index = (j,) + (0,) * len(shape)
    return pl.BlockSpec((None,) + tuple(shape), lambda *_: index, pipeline_mode=pl.Buffered(1))


def _row_spec(tile, width):
    return pl.BlockSpec((None, tile, width), lambda b, s: (b, s, 0))


def _params(semantics):
    return pltpu.CompilerParams(dimension_semantics=semantics,
                                vmem_limit_bytes=V7X_VMEM_LIMIT_BYTES)


def _pool_kernel(*refs, tile, final):
    if final:
        x_ref, g_ref, win_ref, wgrp_ref, scale_ref, wout_ref, fin_ref, o_ref, tail_ref = refs
    else:
        x_ref, g_ref, win_ref, wgrp_ref, scale_ref, wout_ref, o_ref, tail_ref = refs
    s_idx = pl.program_id(1)

    @pl.when(s_idx == 0)
    def _():
        tail_ref[...] = jnp.zeros_like(tail_ref)

    x = x_ref[...]
    xn = _rms_scale(x, g_ref[...]).astype(BF16)
    pos = s_idx * tile + lax.broadcasted_iota(jnp.int32, (tile, 1), 0)
    count = (pos + 1).astype(F32)
    acc = None
    for g, w in enumerate(POOL_WINDOWS):
        lo, hi = g * POOL_GROUP, (g + 1) * POOL_GROUP
        u = _dot(xn, win_ref[:, lo:hi])
        z = _dot(xn, win_ref[:, D_INNER + lo:D_INNER + hi])
        s = jnp.concatenate([tail_ref[:, lo:hi], u], axis=0)
        k = 1
        while k < w:
            s = s + pltpu.roll(s, k, axis=0)
            k *= 2
        s = s[POOL_HALO:, :]
        inv = 1.0 / jnp.minimum(count, float(w))
        pooled = (s * inv - u).astype(BF16)
        mixed = _dot(pooled, wgrp_ref[g]) * scale_ref[:, lo:hi]
        gated = (mixed * _silu(z)).astype(BF16)
        part = _dot(gated, wout_ref[lo:hi, :])
        acc = part if acc is None else acc + part
        tail_ref[:, lo:hi] = u[tile - POOL_HALO:, :]
    y = x + acc
    if final:
        y = _rms_scale(y, fin_ref[...])
    o_ref[...] = y


def _pool_layer(x, j, g, w_in, w_grp, scale, w_out, final_g=None):
    B, S, D = x.shape
    tile = POOL_TILE
    final = final_g is not None
    n_layers = g.shape[0]
    in_specs = [
        _row_spec(tile, D),
        _layer_spec((1, D), j),
        _layer_spec((D, 2 * D_INNER), j),
        _layer_spec((N_POOL_GROUPS, POOL_GROUP, POOL_GROUP), j),
        _layer_spec((1, D_INNER), j),
        _layer_spec((D_INNER, D), j),
    ]
    args = [x, g.reshape(n_layers, 1, D), w_in, w_grp, scale.reshape(n_layers, 1, D_INNER), w_out]
    if final:
        in_specs.append(_const_spec((1, D)))
        args.append(final_g.reshape(1, D))
    return pl.pallas_call(
        functools.partial(_pool_kernel, tile=tile, final=final),
        grid=(B, S // tile),
        in_specs=in_specs,
        out_specs=_row_spec(tile, D),
        out_shape=jax.ShapeDtypeStruct((B, S, D), F32),
        scratch_shapes=[pltpu.VMEM((POOL_HALO, D_INNER), F32)],
        compiler_params=_params(("arbitrary", "arbitrary")),
        name="pool_layer",
    )(*args)


def _conv_kernel(x_ref, g_ref, win_ref, cw_ref, wout_ref, o_ref, tail_ref, *, tile):
    s_idx = pl.program_id(1)

    @pl.when(s_idx == 0)
    def _():
        tail_ref[...] = jnp.zeros_like(tail_ref)

    x = x_ref[...]
    xn = _rms_scale(x, g_ref[...]).astype(BF16)
    acc = None
    for c in range(D_INNER // CONV_CHUNK):
        lo, hi = c * CONV_CHUNK, (c + 1) * CONV_CHUNK
        bb = _dot(xn, win_ref[:, lo:hi])
        cc = _dot(xn, win_ref[:, D_INNER + lo:D_INNER + hi])
        hh = _dot(xn, win_ref[:, 2 * D_INNER + lo:2 * D_INNER + hi])
        zz = _dot(xn, win_ref[:, 3 * D_INNER + lo:3 * D_INNER + hi])
        ch = cc * hh
        che = jnp.concatenate([tail_ref[:, lo:hi], ch], axis=0)
        cw = cw_ref[:, lo:hi]
        conv = (che * cw[2:3, :] + pltpu.roll(che, 1, axis=0) * cw[1:2, :]
                + pltpu.roll(che, 2, axis=0) * cw[0:1, :])[CONV_HALO:, :]
        y = (bb * conv * _silu(zz)).astype(BF16)
        part = _dot(y, wout_ref[lo:hi, :])
        acc = part if acc is None else acc + part
        tail_ref[:, lo:hi] = ch[tile - CONV_HALO:, :]
    o_ref[...] = x + acc


def _conv_layer(x, g, w_in, conv_w, w_out):
    B, S, D = x.shape
    tile = CONV_TILE
    return pl.pallas_call(
        functools.partial(_conv_kernel, tile=tile),
        grid=(B, S // tile),
        in_specs=[
            _row_spec(tile, D),
            _const_spec((1, D)),
            _const_spec((D, 4 * D_INNER)),
            _const_spec((CONV_WIDTH, D_INNER)),
            _const_spec((D_INNER, D)),
        ],
        out_specs=_row_spec(tile, D),
        out_shape=jax.ShapeDtypeStruct((B, S, D), F32),
        scratch_shapes=[pltpu.VMEM((CONV_HALO, D_INNER), F32)],
        compiler_params=_params(("arbitrary", "arbitrary")),
        name="conv_layer",
    )(x, g.reshape(1, D), w_in, conv_w, w_out)


def _rope_constants():
    inv_freq = ROPE_BASE ** (-np.arange(0, QK_ROPE_DIM, 2, dtype=np.float32) / QK_ROPE_DIM)
    lane_group = np.arange(V7X_LANES) // ROPE_HALF
    rows = np.zeros((V7X_SUBLANES, V7X_LANES), np.float32)
    rows[0] = np.tile(inv_freq.astype(np.float32), V7X_LANES // ROPE_HALF)
    rows[1] = np.where(lane_group < 2, -1.0, 1.0)
    rows[2] = (lane_group % 2 == 0)
    rows[3] = (lane_group % 2 == 1)
    return rows


def _mla_column_pieces():
    kr0 = Q_LORA_RANK + KV_LORA_RANK
    w_in_pieces = [(0, kr0),
                   (kr0, ROPE_HALF), (kr0, ROPE_HALF),
                   (kr0 + ROPE_HALF, ROPE_HALF), (kr0 + ROPE_HALF, ROPE_HALF),
                   (kr0 + QK_ROPE_DIM, D_INNER)]
    q_pieces = []
    for p in range(N_PAIRS):
        h0, h1 = 2 * p * Q_HEAD_DIM, (2 * p + 1) * Q_HEAD_DIM
        q_pieces += [(h0, QK_NOPE_DIM),
                     (h0 + QK_NOPE_DIM, ROPE_HALF), (h1 + QK_NOPE_DIM, ROPE_HALF),
                     (h0 + QK_NOPE_DIM + ROPE_HALF, ROPE_HALF),
                     (h1 + QK_NOPE_DIM + ROPE_HALF, ROPE_HALF),
                     (h1, QK_NOPE_DIM)]
    return w_in_pieces, q_pieces


def _take_columns(w, pieces):
    return jnp.concatenate([w[:, a:a + n] for a, n in pieces], axis=1).astype(BF16)


def _mla_proj_kernel(x_ref, pos_ref, g_ref, win_ref, qn_ref, wq_ref, kvn_ref, wkv_ref, rc_ref,
                     q_ref, k_ref, v_ref, z_ref):
    xn = _rms_scale(x_ref[...], g_ref[...]).astype(BF16)
    kv0 = Q_LORA_RANK
    kr0 = kv0 + KV_LORA_RANK
    z0 = kr0 + V7X_LANES
    lat = _dot(xn, win_ref[:, :z0])
    q_lat, kv_lat, kr = lat[:, :kv0], lat[:, kv0:kr0], lat[:, kr0:z0]

    rows = x_ref.shape[0] // PROJ_PIECES
    heads = N_HEADS // PROJ_PIECES
    cos_rows, sin_rows = [], []
    for c in range(PROJ_PIECES):
        z = _dot(xn, win_ref[:, z0 + c * heads * V_HEAD_DIM:z0 + (c + 1) * heads * V_HEAD_DIM])
        for h in range(heads):
            z_ref[c * heads + h] = z[:, h * V_HEAD_DIM:(h + 1) * V_HEAD_DIM]
        angles = pos_ref[c * rows:(c + 1) * rows, :].astype(F32) * rc_ref[0:1, :]
        cos_rows.append(jnp.cos(angles))
        sin_rows.append(jnp.sin(angles) * rc_ref[1:2, :])
    cos = jnp.concatenate(cos_rows, axis=0)
    sin = jnp.concatenate(sin_rows, axis=0)

    def rope(t):
        return t * cos + pltpu.roll(t, 2 * ROPE_HALF, axis=1) * sin

    qn = _rms_scale(q_lat, qn_ref[...]).astype(BF16)
    for p in range(N_PAIRS):
        qp = _dot(qn, wq_ref[:, p * PAIR_Q:(p + 1) * PAIR_Q]) * EXP2_SCALE
        q_rope = rope(qp[:, QK_NOPE_DIM:2 * QK_NOPE_DIM]).astype(BF16)
        q_ref[2 * p, :, :QK_NOPE_DIM] = qp[:, :QK_NOPE_DIM].astype(BF16)
        q_ref[2 * p, :, QK_NOPE_DIM:] = q_rope
        q_ref[2 * p + 1, :, :QK_NOPE_DIM] = q_rope
        q_ref[2 * p + 1, :, QK_NOPE_DIM:] = qp[:, 2 * QK_NOPE_DIM:].astype(BF16)

    kr = rope(kr)
    kr_even = (kr * rc_ref[2:3, :]).astype(BF16)
    kr_odd = (kr * rc_ref[3:4, :]).astype(BF16)
    kvn = _rms_scale(kv_lat, kvn_ref[...]).astype(BF16)
    for h in range(N_HEADS):
        lo = h * HEAD_K
        kv = _dot(kvn, wkv_ref[:, lo:lo + HEAD_K])
        k_nope = kv[:, :QK_NOPE_DIM].astype(BF16)
        v_ref[h] = kv[:, QK_NOPE_DIM:].astype(BF16)
        if h % 2 == 0:
            k_ref[h, :, :QK_NOPE_DIM] = k_nope
            k_ref[h, :, QK_NOPE_DIM:] = kr_even
        else:
            k_ref[h, :, :QK_NOPE_DIM] = kr_odd
            k_ref[h, :, QK_NOPE_DIM:] = k_nope


def _mla_proj(x, positions, g, w_in, q_norm, w_q_up, kv_norm, w_kv_up):
    B, S, D = x.shape
    tile = SEQ_TILE

    def head_spec(width):
        return pl.BlockSpec((None, N_HEADS, tile, width), lambda b, s: (b, 0, s, 0))

    def head_shape(width, dtype):
        return jax.ShapeDtypeStruct((B, N_HEADS, S, width), dtype)

    return pl.pallas_call(
        _mla_proj_kernel,
        grid=(B, S // tile),
        in_specs=[
            _row_spec(tile, D),
            _row_spec(tile, 1),
            _const_spec((1, D)),
            _const_spec((D, MLA_H)),
            _const_spec((1, Q_LORA_RANK)),
            _const_spec((Q_LORA_RANK, N_PAIRS * PAIR_Q)),
            _const_spec((1, KV_LORA_RANK)),
            _const_spec((KV_LORA_RANK, N_HEADS * HEAD_K)),
            _const_spec((V7X_SUBLANES, V7X_LANES)),
        ],
        out_specs=[head_spec(HEAD_K), head_spec(HEAD_K),
                   head_spec(V_HEAD_DIM), head_spec(V_HEAD_DIM)],
        out_shape=[head_shape(HEAD_K, BF16), head_shape(HEAD_K, BF16),
                   head_shape(V_HEAD_DIM, BF16), head_shape(V_HEAD_DIM, F32)],
        compiler_params=_params(("arbitrary", "arbitrary")),
        name="mla_proj",
    )(x, positions.reshape(B, S, 1), g.reshape(1, D), w_in, q_norm.reshape(1, -1), w_q_up,
      kv_norm.reshape(1, -1), w_kv_up, jnp.asarray(_rope_constants()))


def _attn_kernel(q_ref, k_ref, v_ref, z_ref, o_ref, s_even_ref, s_odd_ref, vxt_ref, *, seq):
    nt = (((1,), (1,)), ((), ()))
    n_tiles = seq // ATTN_TQ
    half = ATTN_TQ // 2
    traced_zero = jnp.minimum(pl.program_id(1), 0)
    s_refs = (s_even_ref, s_odd_ref)
    vxt_ref[:V_HEAD_DIM, :] = v_ref[...].astype(F32).T.astype(BF16)
    pad_row = lax.broadcasted_iota(jnp.int32, (V7X_BF16_SUBLANES, seq), 0)
    vxt_ref[V_HEAD_DIM:, :] = jnp.where(pad_row == 0, 1.0, 0.0).astype(BF16)

    def causal(s, first_key, first_query):
        key = first_key + lax.broadcasted_iota(jnp.int32, s.shape, 0)
        query = first_query + lax.broadcasted_iota(jnp.int32, s.shape, 1)
        return jnp.where(key <= query, s, MASK_VALUE)

    def col_max(m, s):
        top = jnp.max(s, axis=0, keepdims=True)
        return top if m is None else jnp.maximum(m, top)

    def scores(i, result):
        s_ref, buf, lo, hi = s_refs[i % 2], traced_zero, i * ATTN_TQ, (i + 1) * ATTN_TQ
        mid = lo + half
        q = q_ref[lo:hi, :]
        m = None
        for c0 in range(0, lo, ATTN_CHUNK):
            c1 = min(c0 + ATTN_CHUNK, lo)
            s = lax.dot_general(k_ref[c0:c1, :], q, nt, preferred_element_type=F32)
            s_ref[buf, c0:c1, :] = s
            m = col_max(m, s)
            yield
        s = causal(lax.dot_general(k_ref[lo:mid, :], q, nt, preferred_element_type=F32), lo, lo)
        s_ref[buf, lo:mid, :] = s
        m = col_max(m, s)
        s = causal(lax.dot_general(k_ref[mid:hi, :], q[half:, :], nt,
                                   preferred_element_type=F32), mid, mid)
        s_ref[buf, mid:hi, half:] = s
        result.append(jnp.concatenate([m[:, :half], col_max(m[:, half:], s)], axis=1))

    def weights(i, mc):
        s_ref, buf, lo, hi = s_refs[i % 2], traced_zero, i * ATTN_TQ, (i + 1) * ATTN_TQ
        mid = lo + half

        def probs(r0, r1, c0, c1):
            return jnp.exp2(s_ref[buf, r0:r1, c0:c1] - mc[:, c0:c1]).astype(BF16)

        acc = None
        for c0 in range(0, mid, ATTN_CHUNK):
            c1 = min(c0 + ATTN_CHUNK, mid)
            part = _dot(vxt_ref[:, c0:c1], probs(c0, c1, 0, ATTN_TQ))
            acc = part if acc is None else acc + part
            yield
        late = _dot(vxt_ref[:, mid:hi], probs(mid, hi, half, ATTN_TQ))
        acc = jnp.concatenate([acc[:, :half], acc[:, half:] + late], axis=1)
        o_t = acc[:V_HEAD_DIM, :] / acc[V_HEAD_DIM:V_HEAD_DIM + 1, :]
        o_ref[lo:hi, :] = (o_t.T * _silu(z_ref[lo:hi, :])).astype(BF16)

    def emit_interleaved(*pieces):
        live = list(pieces)
        while live:
            for piece in list(live):
                if next(piece, "done") == "done":
                    live.remove(piece)

    order = list(range(n_tiles - 1, -1, -1))
    result = []
    emit_interleaved(scores(order[0], result))
    for n, i in enumerate(order):
        mc = result.pop()
        if n + 1 < n_tiles:
            emit_interleaved(scores(order[n + 1], result), weights(i, mc))
        else:
            emit_interleaved(weights(i, mc))


def _attention(q, k, v, z):
    B, _, S, _ = q.shape

    def head_spec(width):
        return pl.BlockSpec((None, None, S, width), lambda b, h: (b, h, 0, 0))

    return pl.pallas_call(
        functools.partial(_attn_kernel, seq=S),
        grid=(B, N_HEADS),
        in_specs=[head_spec(HEAD_K), head_spec(HEAD_K),
                  head_spec(V_HEAD_DIM), head_spec(V_HEAD_DIM)],
        out_specs=pl.BlockSpec((None, S, V_HEAD_DIM), lambda b, h: (b, 0, h)),
        out_shape=jax.ShapeDtypeStruct((B, S, D_INNER), BF16),
        scratch_shapes=[
            pltpu.VMEM((1, S, ATTN_TQ), F32),
            pltpu.VMEM((1, S, ATTN_TQ), F32),
            pltpu.VMEM((V_HEAD_DIM + V7X_BF16_SUBLANES, S), BF16),
        ],
        compiler_params=_params(("arbitrary", "arbitrary")),
        name="mla_attention",
    )(q, k, v, z)


def _out_proj_kernel(x_ref, a_ref, w_ref, o_ref):
    o_ref[...] = x_ref[...] + _dot(a_ref[...], w_ref[...])


def _out_proj(x, a, w_out):
    B, S, D = x.shape
    tile = OUT_PROJ_TILE
    return pl.pallas_call(
        _out_proj_kernel,
        grid=(B, S // tile),
        in_specs=[_row_spec(tile, D), _row_spec(tile, D_INNER), _const_spec((D_INNER, D))],
        out_specs=_row_spec(tile, D),
        out_shape=jax.ShapeDtypeStruct((B, S, D), F32),
        compiler_params=_params(("arbitrary", "arbitrary")),
        name="mla_out_proj",
    )(x, a, w_out)


def _mla_layer(x, positions, g, w_in, q_norm, w_q_up, kv_norm, w_kv_up, w_out):
    w_in_pieces, q_pieces = _mla_column_pieces()
    q, k, v, z = _mla_proj(x, positions, g, _take_columns(w_in, w_in_pieces), q_norm,
                           _take_columns(w_q_up, q_pieces), kv_norm, w_kv_up.astype(BF16))
    gated = _attention(q, k, v, z)
    return _out_proj(x, gated, w_out.astype(BF16))


def kernel(x, positions, pool_norm, pool_w_in, pool_w_grp, pool_scale, pool_w_out,
           conv_norm, conv_w_in, conv_w, conv_w_out,
           mla_norm, mla_w_in, mla_q_norm, mla_w_q_up, mla_kv_norm, mla_w_kv_up, mla_w_out,
           final_norm):
    pool_weights = (pool_w_in.astype(BF16), pool_w_grp.astype(BF16), pool_scale,
                    pool_w_out.astype(BF16))
    for i in range(DEPTH):
        kind, j = i % N_MIXERS, i // N_MIXERS
        last = i == DEPTH - 1
        if kind == 0:
            x = _pool_layer(x, j, pool_norm, *pool_weights, final_norm if last else None)
        elif kind == 1:
            x = _conv_layer(x, conv_norm[j], conv_w_in[j].astype(BF16), conv_w[j],
                            conv_w_out[j].astype(BF16))
        else:
            x = _mla_layer(x, positions, mla_norm[j], mla_w_in[j], mla_q_norm[j],
                           mla_w_q_up[j], mla_kv_norm[j], mla_w_kv_up[j], mla_w_out[j])
        if last and kind != 0:
            raise NotImplementedError("final norm is fused into a pool layer")
    return x
```

```python
import functools
import math

import numpy as np
import jax
import jax.numpy as jnp
from jax import lax
from jax.experimental import pallas as pl
from jax.experimental.pallas import tpu as pltpu

D_MODEL = 1024
DEPTH = 4
N_MIXERS = 3
D_INNER = 2 * D_MODEL
POOL_WINDOWS = (2, 4, 8, 16)
N_POOL_GROUPS = len(POOL_WINDOWS)
POOL_GROUP = D_INNER // N_POOL_GROUPS
CONV_WIDTH = 3
N_HEADS = 16
N_PAIRS = N_HEADS // 2
QK_NOPE_DIM = 128
QK_ROPE_DIM = 64
ROPE_HALF = QK_ROPE_DIM // 2
V_HEAD_DIM = D_INNER // N_HEADS
Q_LORA_RANK = 384
KV_LORA_RANK = 256
Q_HEAD_DIM = QK_NOPE_DIM + QK_ROPE_DIM
ATTN_SCALE = Q_HEAD_DIM ** -0.5
ROPE_BASE = 10000.0
NORM_EPS = 1e-6
MASK_VALUE = -1e30

V7X_LANES = 128
V7X_SUBLANES = 8
V7X_BF16_SUBLANES = 16
V7X_VMEM_LIMIT_BYTES = 56 * 1024 * 1024

POOL_HALO = max(POOL_WINDOWS)
CONV_HALO = V7X_SUBLANES
SEQ_TILE = 512
PROJ_PIECES = 4
POOL_TILE = 1024
CONV_TILE = 1024
OUT_PROJ_TILE = 1024
CONV_CHUNK = 512
ATTN_TQ = 512
ATTN_CHUNK = 512
EXP2_SCALE = ATTN_SCALE * math.log2(math.e)
HEAD_K = 2 * QK_NOPE_DIM

F32 = jnp.float32
BF16 = jnp.bfloat16


def _rms_scale(x, g):
    ms = jnp.mean(x * x, axis=-1, keepdims=True)
    return x * lax.rsqrt(ms + NORM_EPS) * g


def _silu(z):
    return z * (1.0 / (1.0 + jnp.exp(-z)))


def _dot(a, b):
    return jnp.dot(a, b, preferred_element_type=F32)


def _const_spec(shape):
    zeros = (0,) * len(shape)
    return pl.BlockSpec(shape, lambda *_: zeros, pipeline_mode=pl.Buffered(1))


def _layer_spec(shape, j):
    index = (j,) + (0,) * len(shape)
    return pl.BlockSpec((None,) + tuple(shape), lambda *_: index, pipeline_mode=pl.Buffered(1))


def _row_spec(tile, width):
    return pl.BlockSpec((None, tile, width), lambda b, s: (b, s, 0))


def _params(semantics):
    return pltpu.CompilerParams(dimension_semantics=semantics,
                                vmem_limit_bytes=V7X_VMEM_LIMIT_BYTES)


def _pool_kernel(*refs, tile, final):
    if final:
        x_ref, g_ref, win_ref, wgrp_ref, scale_ref, wout_ref, fin_ref, o_ref, tail_ref = refs
    else:
        x_ref, g_ref, win_ref, wgrp_ref, scale_ref, wout_ref, o_ref, tail_ref = refs
    s_idx = pl.program_id(1)

    @pl.when(s_idx == 0)
    def _():
        tail_ref[...] = jnp.zeros_like(tail_ref)

    x = x_ref[...]
    xn = _rms_scale(x, g_ref[...]).astype(BF16)
    pos = s_idx * tile + lax.broadcasted_iota(jnp.int32, (tile, 1), 0)
    count = (pos + 1).astype(F32)
    acc = None
    for g, w in enumerate(POOL_WINDOWS):
        lo, hi = g * POOL_GROUP, (g + 1) * POOL_GROUP
        u = _dot(xn, win_ref[:, lo:hi])
        z = _dot(xn, win_ref[:, D_INNER + lo:D_INNER + hi])
        s = jnp.concatenate([tail_ref[:, lo:hi], u], axis=0)
        k = 1
        while k < w:
            s = s + pltpu.roll(s, k, axis=0)
            k *= 2
        s = s[POOL_HALO:, :]
        inv = 1.0 / jnp.minimum(count, float(w))
        pooled = (s * inv - u).astype(BF16)
        mixed = _dot(pooled, wgrp_ref[g]) * scale_ref[:, lo:hi]
        gated = (mixed * _silu(z)).astype(BF16)
        part = _dot(gated, wout_ref[lo:hi, :])
        acc = part if acc is None else acc + part
        tail_ref[:, lo:hi] = u[tile - POOL_HALO:, :]
    y = x + acc
    if final:
        y = _rms_scale(y, fin_ref[...])
    o_ref[...] = y


def _pool_layer(x, j, g, w_in, w_grp, scale, w_out, final_g=None):
    B, S, D = x.shape
    tile = POOL_TILE
    final = final_g is not None
    n_layers = g.shape[0]
    in_specs = [
        _row_spec(tile, D),
        _layer_spec((1, D), j),
        _layer_spec((D, 2 * D_INNER), j),
        _layer_spec((N_POOL_GROUPS, POOL_GROUP, POOL_GROUP), j),
        _layer_spec((1, D_INNER), j),
        _layer_spec((D_INNER, D), j),
    ]
    args = [x, g.reshape(n_layers, 1, D), w_in, w_grp, scale.reshape(n_layers, 1, D_INNER), w_out]
    if final:
        in_specs.append(_const_spec((1, D)))
        args.append(final_g.reshape(1, D))
    return pl.pallas_call(
        functools.partial(_pool_kernel, tile=tile, final=final),
        grid=(B, S // tile),
        in_specs=in_specs,
        out_specs=_row_spec(tile, D),
        out_shape=jax.ShapeDtypeStruct((B, S, D), F32),
        scratch_shapes=[pltpu.VMEM((POOL_HALO, D_INNER), F32)],
        compiler_params=_params(("arbitrary", "arbitrary")),
        name="pool_layer",
    )(*args)


def _conv_kernel(x_ref, g_ref, win_ref, cw_ref, wout_ref, o_ref, tail_ref, *, tile):
    s_idx = pl.program_id(1)

    @pl.when(s_idx == 0)
    def _():
        tail_ref[...] = jnp.zeros_like(tail_ref)

    x = x_ref[...]
    xn = _rms_scale(x, g_ref[...]).astype(BF16)
    acc = None
    for c in range(D_INNER // CONV_CHUNK):
        lo, hi = c * CONV_CHUNK, (c + 1) * CONV_CHUNK
        bb = _dot(xn, win_ref[:, lo:hi])
        cc = _dot(xn, win_ref[:, D_INNER + lo:D_INNER + hi])
        hh = _dot(xn, win_ref[:, 2 * D_INNER + lo:2 * D_INNER + hi])
        zz = _dot(xn, win_ref[:, 3 * D_INNER + lo:3 * D_INNER + hi])
        ch = cc * hh
        che = jnp.concatenate([tail_ref[:, lo:hi], ch], axis=0)
        cw = cw_ref[:, lo:hi]
        conv = (che * cw[2:3, :] + pltpu.roll(che, 1, axis=0) * cw[1:2, :]
                + pltpu.roll(che, 2, axis=0) * cw[0:1, :])[CONV_HALO:, :]
        y = (bb * conv * _silu(zz)).astype(BF16)
        part = _dot(y, wout_ref[lo:hi, :])
        acc = part if acc is None else acc + part
        tail_ref[:, lo:hi] = ch[tile - CONV_HALO:, :]
    o_ref[...] = x + acc


def _conv_layer(x, g, w_in, conv_w, w_out):
    B, S, D = x.shape
    tile = CONV_TILE
    return pl.pallas_call(
        functools.partial(_conv_kernel, tile=tile),
        grid=(B, S // tile),
        in_specs=[
            _row_spec(tile, D),
            _const_spec((1, D)),
            _const_spec((D, 4 * D_INNER)),
            _const_spec((CONV_WIDTH, D_INNER)),
            _const_spec((D_INNER, D)),
        ],
        out_specs=_row_spec(tile, D),
        out_shape=jax.ShapeDtypeStruct((B, S, D), F32),
        scratch_shapes=[pltpu.VMEM((CONV_HALO, D_INNER), F32)],
        compiler_params=_params(("arbitrary", "arbitrary")),
        name="conv_layer",
    )(x, g.reshape(1, D), w_in, conv_w, w_out)


def _rope_constants():
    inv_freq = ROPE_BASE ** (-np.arange(0, QK_ROPE_DIM, 2, dtype=np.float32) / QK_ROPE_DIM)
    lane_group = np.arange(V7X_LANES) // ROPE_HALF
    rows = np.zeros((V7X_SUBLANES, V7X_LANES), np.float32)
    rows[0] = np.tile(inv_freq.astype(np.float32), V7X_LANES // ROPE_HALF)
    rows[1] = np.where(lane_group < 2, -1.0, 1.0)
    rows[2] = (lane_group % 2 == 0)
    rows[3] = (lane_group % 2 == 1)
    return rows


def _mla_proj_weights(w_in, w_q_up):
    d = w_in.shape[0]
    kr0 = Q_LORA_RANK + KV_LORA_RANK
    k_rope = w_in[:, kr0:kr0 + QK_ROPE_DIM].reshape(d, 2, 1, ROPE_HALF)
    k_rope = jnp.broadcast_to(k_rope, (d, 2, 2, ROPE_HALF)).reshape(d, 2 * QK_ROPE_DIM)
    w_lat = jnp.concatenate([w_in[:, :kr0], k_rope], axis=1).astype(BF16)
    w_gate = w_in[:, kr0 + QK_ROPE_DIM:].astype(BF16)
    per_head = w_q_up.reshape(Q_LORA_RANK, N_HEADS, Q_HEAD_DIM)
    w_q_nope = per_head[:, :, :QK_NOPE_DIM].reshape(Q_LORA_RANK, N_HEADS * QK_NOPE_DIM)
    rope = per_head[:, :, QK_NOPE_DIM:].reshape(Q_LORA_RANK, N_PAIRS, 2, 2, ROPE_HALF)
    w_q_rope = rope.transpose(0, 1, 3, 2, 4).reshape(Q_LORA_RANK, N_PAIRS * 2 * QK_ROPE_DIM)
    return w_lat, w_gate, w_q_nope.astype(BF16), w_q_rope.astype(BF16)


def _mla_proj_kernel(x_ref, pos_ref, g_ref, wlat_ref, wgate_ref, qn_ref, wqn_ref, wqr_ref,
                     kvn_ref, wkv_ref, rc_ref, q_ref, k_ref, v_ref, z_ref):
    tile = x_ref.shape[0]
    xn = _rms_scale(x_ref[...], g_ref[...]).astype(BF16)
    kv0 = Q_LORA_RANK
    kr0 = kv0 + KV_LORA_RANK
    lat = _dot(xn, wlat_ref[...])
    q_lat, kv_lat, kr = lat[:, :kv0], lat[:, kv0:kr0], lat[:, kr0:]

    pos_row = pos_ref[...].astype(F32)
    pos_col = jnp.broadcast_to(pos_row, (V7X_SUBLANES, tile)).T[:, 0:1]

    rows = tile // PROJ_PIECES
    heads = N_HEADS // PROJ_PIECES
    cos_rows, sin_rows = [], []
    for c in range(PROJ_PIECES):
        z = _dot(xn, wgate_ref[:, c * heads * V_HEAD_DIM:(c + 1) * heads * V_HEAD_DIM])
        for h in range(heads):
            z_ref[c * heads + h] = z[:, h * V_HEAD_DIM:(h + 1) * V_HEAD_DIM]
        angles = pos_col[c * rows:(c + 1) * rows, :] * rc_ref[0:1, :]
        cos_rows.append(jnp.cos(angles))
        sin_rows.append(jnp.sin(angles) * rc_ref[1:2, :])
    cos = jnp.concatenate(cos_rows, axis=0)
    sin = jnp.concatenate(sin_rows, axis=0)

    def rope(t):
        return t * cos + pltpu.roll(t, 2 * ROPE_HALF, axis=1) * sin

    qn = _rms_scale(q_lat, qn_ref[...]).astype(BF16)
    q_nope = _dot(qn, wqn_ref[...]) * EXP2_SCALE
    q_rope = _dot(qn, wqr_ref[...]) * EXP2_SCALE
    for p in range(N_PAIRS):
        pair_rope = rope(q_rope[:, p * V7X_LANES:(p + 1) * V7X_LANES]).astype(BF16)
        for h in (2 * p, 2 * p + 1):
            nope = q_nope[:, h * QK_NOPE_DIM:(h + 1) * QK_NOPE_DIM].astype(BF16)
            if h % 2 == 0:
                q_ref[h, :, :QK_NOPE_DIM] = nope
                q_ref[h, :, QK_NOPE_DIM:] = pair_rope
            else:
                q_ref[h, :, :QK_NOPE_DIM] = pair_rope
                q_ref[h, :, QK_NOPE_DIM:] = nope

    kr = rope(kr)
    kr_even = (kr * rc_ref[2:3, :]).astype(BF16)
    kr_odd = (kr * rc_ref[3:4, :]).astype(BF16)
    kvn = _rms_scale(kv_lat, kvn_ref[...]).astype(BF16)
    for c in range(PROJ_PIECES):
        kv = _dot(kvn, wkv_ref[:, c * heads * HEAD_K:(c + 1) * heads * HEAD_K])
        for h in range(c * heads, (c + 1) * heads):
            lo = (h - c * heads) * HEAD_K
            k_nope = kv[:, lo:lo + QK_NOPE_DIM].astype(BF16)
            v_ref[h] = kv[:, lo + QK_NOPE_DIM:lo + HEAD_K].astype(BF16)
            if h % 2 == 0:
                k_ref[h, :, :QK_NOPE_DIM] = k_nope
                k_ref[h, :, QK_NOPE_DIM:] = kr_even
            else:
                k_ref[h, :, :QK_NOPE_DIM] = kr_odd
                k_ref[h, :, QK_NOPE_DIM:] = k_nope


def _mla_proj(x, positions, g, w_in, q_norm, w_q_up, kv_norm, w_kv_up):
    B, S, D = x.shape
    tile = SEQ_TILE
    w_lat, w_gate, w_q_nope, w_q_rope = _mla_proj_weights(w_in, w_q_up)

    def head_spec(width):
        return pl.BlockSpec((None, N_HEADS, tile, width), lambda b, s: (b, 0, s, 0))

    def head_shape(width, dtype):
        return jax.ShapeDtypeStruct((B, N_HEADS, S, width), dtype)

    return pl.pallas_call(
        _mla_proj_kernel,
        grid=(B, S // tile),
        in_specs=[
            _row_spec(tile, D),
            pl.BlockSpec((None, 1, tile), lambda b, s: (b, 0, s)),
            _const_spec((1, D)),
            _const_spec(w_lat.shape),
            _const_spec(w_gate.shape),
            _const_spec((1, Q_LORA_RANK)),
            _const_spec(w_q_nope.shape),
            _const_spec(w_q_rope.shape),
            _const_spec((1, KV_LORA_RANK)),
            _const_spec((KV_LORA_RANK, N_HEADS * HEAD_K)),
            _const_spec((V7X_SUBLANES, V7X_LANES)),
        ],
        out_specs=[head_spec(HEAD_K), head_spec(HEAD_K),
                   head_spec(V_HEAD_DIM), head_spec(V_HEAD_DIM)],
        out_shape=[head_shape(HEAD_K, BF16), head_shape(HEAD_K, BF16),
                   head_shape(V_HEAD_DIM, BF16), head_shape(V_HEAD_DIM, F32)],
        compiler_params=_params(("arbitrary", "arbitrary")),
        name="mla_proj",
    )(x, positions.reshape(B, 1, S), g.reshape(1, D), w_lat, w_gate, q_norm.reshape(1, -1),
      w_q_nope, w_q_rope, kv_norm.reshape(1, -1), w_kv_up.astype(BF16),
      jnp.asarray(_rope_constants()))


def _attn_kernel(q_ref, k_ref, v_ref, z_ref, o_ref, s_even_ref, s_odd_ref, vxt_ref, *, seq):
    nt = (((1,), (1,)), ((), ()))
    n_tiles = seq // ATTN_TQ
    half = ATTN_TQ // 2
    traced_zero = jnp.minimum(pl.program_id(1), 0)
    s_refs = (s_even_ref, s_odd_ref)
    vxt_ref[:V_HEAD_DIM, :] = v_ref[...].astype(F32).T.astype(BF16)
    pad_row = lax.broadcasted_iota(jnp.int32, (V7X_BF16_SUBLANES, seq), 0)
    vxt_ref[V_HEAD_DIM:, :] = jnp.where(pad_row == 0, 1.0, 0.0).astype(BF16)

    def causal(s, first_key, first_query):
        key = first_key + lax.broadcasted_iota(jnp.int32, s.shape, 0)
        query = first_query + lax.broadcasted_iota(jnp.int32, s.shape, 1)
        return jnp.where(key <= query, s, MASK_VALUE)

    def col_max(m, s):
        top = jnp.max(s, axis=0, keepdims=True)
        return top if m is None else jnp.maximum(m, top)

    def scores(i, result):
        s_ref, buf, lo, hi = s_refs[i % 2], traced_zero, i * ATTN_TQ, (i + 1) * ATTN_TQ
        mid = lo + half
        q = q_ref[lo:hi, :]
        m = None
        for c0 in range(0, lo, ATTN_CHUNK):
            c1 = min(c0 + ATTN_CHUNK, lo)
            s = lax.dot_general(k_ref[c0:c1, :], q, nt, preferred_element_type=F32)
            s_ref[buf, c0:c1, :] = s
            m = col_max(m, s)
            yield
        s = causal(lax.dot_general(k_ref[lo:mid, :], q, nt, preferred_element_type=F32), lo, lo)
        s_ref[buf, lo:mid, :] = s
        m = col_max(m, s)
        s = causal(lax.dot_general(k_ref[mid:hi, :], q[half:, :], nt,
                                   preferred_element_type=F32), mid, mid)
        s_ref[buf, mid:hi, half:] = s
        result.append(jnp.concatenate([m[:, :half], col_max(m[:, half:], s)], axis=1))

    def weights(i, mc):
        s_ref, buf, lo, hi = s_refs[i % 2], traced_zero, i * ATTN_TQ, (i + 1) * ATTN_TQ
        mid = lo + half

        def probs(r0, r1, c0, c1):
            return jnp.exp2(s_ref[buf, r0:r1, c0:c1] - mc[:, c0:c1]).astype(BF16)

        acc = None
        for c0 in range(0, mid, ATTN_CHUNK):
            c1 = min(c0 + ATTN_CHUNK, mid)
            part = _dot(vxt_ref[:, c0:c1], probs(c0, c1, 0, ATTN_TQ))
            acc = part if acc is None else acc + part
            yield
        late = _dot(vxt_ref[:, mid:hi], probs(mid, hi, half, ATTN_TQ))
        acc = jnp.concatenate([acc[:, :half], acc[:, half:] + late], axis=1)
        o_t = acc[:V_HEAD_DIM, :] / acc[V_HEAD_DIM:V_HEAD_DIM + 1, :]
        o_ref[lo:hi, :] = (o_t.T * _silu(z_ref[lo:hi, :])).astype(BF16)

    def emit_interleaved(*pieces):
        live = list(pieces)
        while live:
            for piece in list(live):
                if next(piece, "done") == "done":
                    live.remove(piece)

    order = list(range(n_tiles - 1, -1, -1))
    result = []
    emit_interleaved(scores(order[0], result))
    for n, i in enumerate(order):
        mc = result.pop()
        if n + 1 < n_tiles:
            emit_interleaved(scores(order[n + 1], result), weights(i, mc))
        else:
            emit_interleaved(weights(i, mc))


def _attention(q, k, v, z):
    B, _, S, _ = q.shape

    def head_spec(width):
        return pl.BlockSpec((None, None, S, width), lambda b, h: (b, h, 0, 0))

    return pl.pallas_call(
        functools.partial(_attn_kernel, seq=S),
        grid=(B, N_HEADS),
        in_specs=[head_spec(HEAD_K), head_spec(HEAD_K),
                  head_spec(V_HEAD_DIM), head_spec(V_HEAD_DIM)],
        out_specs=pl.BlockSpec((None, S, V_HEAD_DIM), lambda b, h: (b, 0, h)),
        out_shape=jax.ShapeDtypeStruct((B, S, D_INNER), BF16),
        scratch_shapes=[
            pltpu.VMEM((1, S, ATTN_TQ), F32),
            pltpu.VMEM((1, S, ATTN_TQ), F32),
            pltpu.VMEM((V_HEAD_DIM + V7X_BF16_SUBLANES, S), BF16),
        ],
        compiler_params=_params(("arbitrary", "arbitrary")),
        name="mla_attention",
    )(q, k, v, z)


def _out_proj_kernel(x_ref, a_ref, w_ref, o_ref):
    o_ref[...] = x_ref[...] + _dot(a_ref[...], w_ref[...])


def _out_proj(x, a, w_out):
    B, S, D = x.shape
    tile = OUT_PROJ_TILE
    return pl.pallas_call(
        _out_proj_kernel,
        grid=(B, S // tile),
        in_specs=[_row_spec(tile, D), _row_spec(tile, D_INNER), _const_spec((D_INNER, D))],
        out_specs=_row_spec(tile, D),
        out_shape=jax.ShapeDtypeStruct((B, S, D), F32),
        compiler_params=_params(("arbitrary", "arbitrary")),
        name="mla_out_proj",
    )(x, a, w_out)


def _mla_layer(x, positions, g, w_in, q_norm, w_q_up, kv_norm, w_kv_up, w_out):
    q, k, v, z = _mla_proj(x, positions, g, w_in, q_norm, w_q_up, kv_norm, w_kv_up)
    gated = _attention(q, k, v, z)
    return _out_proj(x, gated, w_out.astype(BF16))


def kernel(x, positions, pool_norm, pool_w_in, pool_w_grp, pool_scale, pool_w_out,
           conv_norm, conv_w_in, conv_w, conv_w_out,
           mla_norm, mla_w_in, mla_q_norm, mla_w_q_up, mla_kv_norm, mla_w_kv_up, mla_w_out,
           final_norm):
    assert (DEPTH - 1) % N_MIXERS == 0, "the final norm is fused into a closing pool layer"
    pool_weights = (pool_w_in.astype(BF16), pool_w_grp.astype(BF16), pool_scale,
                    pool_w_out.astype(BF16))
    for i in range(DEPTH):
        kind, j = i % N_MIXERS, i // N_MIXERS
        last = i == DEPTH - 1
        if kind == 0:
            x = _pool_layer(x, j, pool_norm, *pool_weights, final_norm if last else None)
        elif kind == 1:
            x = _conv_layer(x, conv_norm[j], conv_w_in[j].astype(BF16), conv_w[j],
                            conv_w_out[j].astype(BF16))
        else:
            x = _mla_layer(x, positions, mla_norm[j], mla_w_in[j], mla_q_norm[j],
                           mla_w_q_up[j], mla_kv_norm[j], mla_w_kv_up[j], mla_w_out[j])
    return x
```

```python
import functools
import math

import numpy as np
import jax
import jax.numpy as jnp
from jax import lax
from jax.experimental import pallas as pl
from jax.experimental.pallas import tpu as pltpu

D_MODEL = 1024
DEPTH = 4
N_MIXERS = 3
D_INNER = 2 * D_MODEL
POOL_WINDOWS = (2, 4, 8, 16)
N_POOL_GROUPS = len(POOL_WINDOWS)
POOL_GROUP = D_INNER // N_POOL_GROUPS
CONV_WIDTH = 3
N_HEADS = 16
N_PAIRS = N_HEADS // 2
QK_NOPE_DIM = 128
QK_ROPE_DIM = 64
ROPE_HALF = QK_ROPE_DIM // 2
V_HEAD_DIM = D_INNER // N_HEADS
Q_LORA_RANK = 384
KV_LORA_RANK = 256
Q_HEAD_DIM = QK_NOPE_DIM + QK_ROPE_DIM
ATTN_SCALE = Q_HEAD_DIM ** -0.5
ROPE_BASE = 10000.0
NORM_EPS = 1e-6
MASK_VALUE = -1e30

V7X_LANES = 128
V7X_SUBLANES = 8
V7X_BF16_SUBLANES = 16
V7X_VMEM_LIMIT_BYTES = 56 * 1024 * 1024

POOL_HALO = max(POOL_WINDOWS)
CONV_HALO = V7X_SUBLANES
SEQ_TILE = 512
PROJ_PIECES = 4
POOL_TILE = 1024
CONV_TILE = 1024
POOL_LOAD_STEPS = 8
CONV_LOAD_STEPS = 16
OUT_PROJ_TILE = 1024
CONV_CHUNK = 512
ATTN_TQ = 512
ATTN_CHUNK = 512
EXP2_SCALE = ATTN_SCALE * math.log2(math.e)
HEAD_K = 2 * QK_NOPE_DIM

F32 = jnp.float32
BF16 = jnp.bfloat16


def _rms_scale(x, g):
    ms = jnp.mean(x * x, axis=-1, keepdims=True)
    return x * lax.rsqrt(ms + NORM_EPS) * g


def _silu(z):
    return z * (1.0 / (1.0 + jnp.exp(-z)))


def _dot(a, b):
    return jnp.dot(a, b, preferred_element_type=F32)


def _const_spec(shape):
    zeros = (0,) * len(shape)
    return pl.BlockSpec(shape, lambda *_: zeros, pipeline_mode=pl.Buffered(1))


def _layer_spec(shape, j):
    index = (j,) + (0,) * len(shape)
    return pl.BlockSpec((None,) + tuple(shape), lambda *_: index, pipeline_mode=pl.Buffered(1))


def _row_spec(tile, width):
    return pl.BlockSpec((None, tile, width), lambda b, s: (b, s, 0))


def _params(semantics):
    return pltpu.CompilerParams(dimension_semantics=semantics,
                                vmem_limit_bytes=V7X_VMEM_LIMIT_BYTES)


def _phase_row_spec(tile, width, n_seq, n_load):
    def index(t):
        c = jnp.maximum(t - n_load, 0)
        return (c // n_seq, c % n_seq, 0)
    return pl.BlockSpec((None, tile, width), index)


def _weight_chunk_spec(rows, cols, j, n_load):
    return pl.BlockSpec((None, rows // n_load, cols),
                        lambda t: (j, jnp.minimum(t, n_load - 1), 0))


def _store_weight_chunk(dst_ref, src_ref, t):
    chunk = src_ref.shape[0]
    dst_ref[pl.ds(pl.multiple_of(t * chunk, chunk), chunk), :] = src_ref[...].astype(BF16)


def _pool_kernel(*refs, tile, n_seq, final):
    if final:
        (x_ref, g_ref, win_f32, wgrp_f32, scale_ref, wout_f32, fin_ref, o_ref,
         win_ref, wgrp_ref, wout_ref, tail_ref) = refs
    else:
        (x_ref, g_ref, win_f32, wgrp_f32, scale_ref, wout_f32, o_ref,
         win_ref, wgrp_ref, wout_ref, tail_ref) = refs
    t = pl.program_id(0)

    @pl.when(t < POOL_LOAD_STEPS)
    def _():
        _store_weight_chunk(win_ref, win_f32, t)
        _store_weight_chunk(wgrp_ref, wgrp_f32, t)
        _store_weight_chunk(wout_ref, wout_f32, t)

    @pl.when(t >= POOL_LOAD_STEPS)
    def _():
        s_idx = lax.rem(t - POOL_LOAD_STEPS, n_seq)

        @pl.when(s_idx == 0)
        def _():
            tail_ref[...] = jnp.zeros_like(tail_ref)

        x = x_ref[...]
        xn = _rms_scale(x, g_ref[...]).astype(BF16)
        pos = s_idx * tile + lax.broadcasted_iota(jnp.int32, (tile, 1), 0)
        count = (pos + 1).astype(F32)
        acc = None
        for g, w in enumerate(POOL_WINDOWS):
            lo, hi = g * POOL_GROUP, (g + 1) * POOL_GROUP
            u = _dot(xn, win_ref[:, lo:hi])
            z = _dot(xn, win_ref[:, D_INNER + lo:D_INNER + hi])
            s = jnp.concatenate([tail_ref[:, lo:hi], u], axis=0)
            k = 1
            while k < w:
                s = s + pltpu.roll(s, k, axis=0)
                k *= 2
            s = s[POOL_HALO:, :]
            inv = 1.0 / jnp.minimum(count, float(w))
            pooled = (s * inv - u).astype(BF16)
            mixed = _dot(pooled, wgrp_ref[lo:hi, :]) * scale_ref[:, lo:hi]
            gated = (mixed * _silu(z)).astype(BF16)
            part = _dot(gated, wout_ref[lo:hi, :])
            acc = part if acc is None else acc + part
            tail_ref[:, lo:hi] = u[tile - POOL_HALO:, :]
        y = x + acc
        if final:
            y = _rms_scale(y, fin_ref[...])
        o_ref[...] = y


def _pool_layer(x, j, g, w_in, w_grp, scale, w_out, final_g=None):
    B, S, D = x.shape
    tile = POOL_TILE
    n_seq = S // tile
    final = final_g is not None
    n_layers = g.shape[0]
    grp_rows = N_POOL_GROUPS * POOL_GROUP
    in_specs = [
        _phase_row_spec(tile, D, n_seq, POOL_LOAD_STEPS),
        _layer_spec((1, D), j),
        _weight_chunk_spec(D, 2 * D_INNER, j, POOL_LOAD_STEPS),
        _weight_chunk_spec(grp_rows, POOL_GROUP, j, POOL_LOAD_STEPS),
        _layer_spec((1, D_INNER), j),
        _weight_chunk_spec(D_INNER, D, j, POOL_LOAD_STEPS),
    ]
    args = [x, g.reshape(n_layers, 1, D), w_in, w_grp.reshape(n_layers, grp_rows, POOL_GROUP),
            scale.reshape(n_layers, 1, D_INNER), w_out]
    if final:
        in_specs.append(_const_spec((1, D)))
        args.append(final_g.reshape(1, D))
    return pl.pallas_call(
        functools.partial(_pool_kernel, tile=tile, n_seq=n_seq, final=final),
        grid=(POOL_LOAD_STEPS + B * n_seq,),
        in_specs=in_specs,
        out_specs=_phase_row_spec(tile, D, n_seq, POOL_LOAD_STEPS),
        out_shape=jax.ShapeDtypeStruct((B, S, D), F32),
        scratch_shapes=[pltpu.VMEM((D, 2 * D_INNER), BF16),
                        pltpu.VMEM((grp_rows, POOL_GROUP), BF16),
                        pltpu.VMEM((D_INNER, D), BF16),
                        pltpu.VMEM((POOL_HALO, D_INNER), F32)],
        compiler_params=_params(("arbitrary",)),
        name="pool_layer",
    )(*args)


def _conv_kernel(x_ref, g_ref, win_f32, cw_ref, wout_f32, o_ref, win_ref, wout_ref, tail_ref,
                 *, tile, n_seq):
    t = pl.program_id(0)

    @pl.when(t < CONV_LOAD_STEPS)
    def _():
        _store_weight_chunk(win_ref, win_f32, t)
        _store_weight_chunk(wout_ref, wout_f32, t)

    @pl.when(t >= CONV_LOAD_STEPS)
    def _():
        s_idx = lax.rem(t - CONV_LOAD_STEPS, n_seq)

        @pl.when(s_idx == 0)
        def _():
            tail_ref[...] = jnp.zeros_like(tail_ref)

        x = x_ref[...]
        xn = _rms_scale(x, g_ref[...]).astype(BF16)
        acc = None
        for c in range(D_INNER // CONV_CHUNK):
            lo, hi = c * CONV_CHUNK, (c + 1) * CONV_CHUNK
            bb = _dot(xn, win_ref[:, lo:hi])
            cc = _dot(xn, win_ref[:, D_INNER + lo:D_INNER + hi])
            hh = _dot(xn, win_ref[:, 2 * D_INNER + lo:2 * D_INNER + hi])
            zz = _dot(xn, win_ref[:, 3 * D_INNER + lo:3 * D_INNER + hi])
            ch = cc * hh
            che = jnp.concatenate([tail_ref[:, lo:hi], ch], axis=0)
            cw = cw_ref[:, lo:hi]
            conv = (che * cw[2:3, :] + pltpu.roll(che, 1, axis=0) * cw[1:2, :]
                    + pltpu.roll(che, 2, axis=0) * cw[0:1, :])[CONV_HALO:, :]
            y = (bb * conv * _silu(zz)).astype(BF16)
            part = _dot(y, wout_ref[lo:hi, :])
            acc = part if acc is None else acc + part
            tail_ref[:, lo:hi] = ch[tile - CONV_HALO:, :]
        o_ref[...] = x + acc


def _conv_layer(x, j, g, w_in, conv_w, w_out):
    B, S, D = x.shape
    tile = CONV_TILE
    n_seq = S // tile
    n_layers = g.shape[0]
    return pl.pallas_call(
        functools.partial(_conv_kernel, tile=tile, n_seq=n_seq),
        grid=(CONV_LOAD_STEPS + B * n_seq,),
        in_specs=[
            _phase_row_spec(tile, D, n_seq, CONV_LOAD_STEPS),
            _layer_spec((1, D), j),
            _weight_chunk_spec(D, 4 * D_INNER, j, CONV_LOAD_STEPS),
            _layer_spec((CONV_WIDTH, D_INNER), j),
            _weight_chunk_spec(D_INNER, D, j, CONV_LOAD_STEPS),
        ],
        out_specs=_phase_row_spec(tile, D, n_seq, CONV_LOAD_STEPS),
        out_shape=jax.ShapeDtypeStruct((B, S, D), F32),
        scratch_shapes=[pltpu.VMEM((D, 4 * D_INNER), BF16),
                        pltpu.VMEM((D_INNER, D), BF16),
                        pltpu.VMEM((CONV_HALO, D_INNER), F32)],
        compiler_params=_params(("arbitrary",)),
        name="conv_layer",
    )(x, g.reshape(n_layers, 1, D), w_in, conv_w, w_out)


def _rope_constants():
    inv_freq = ROPE_BASE ** (-np.arange(0, QK_ROPE_DIM, 2, dtype=np.float32) / QK_ROPE_DIM)
    lane_group = np.arange(V7X_LANES) // ROPE_HALF
    rows = np.zeros((V7X_SUBLANES, V7X_LANES), np.float32)
    rows[0] = np.tile(inv_freq.astype(np.float32), V7X_LANES // ROPE_HALF)
    rows[1] = np.where(lane_group < 2, -1.0, 1.0)
    rows[2] = (lane_group % 2 == 0)
    rows[3] = (lane_group % 2 == 1)
    return rows


def _mla_proj_weights(w_in, w_q_up):
    d = w_in.shape[0]
    kr0 = Q_LORA_RANK + KV_LORA_RANK
    k_rope = w_in[:, kr0:kr0 + QK_ROPE_DIM].reshape(d, 2, 1, ROPE_HALF)
    k_rope = jnp.broadcast_to(k_rope, (d, 2, 2, ROPE_HALF)).reshape(d, 2 * QK_ROPE_DIM)
    w_lat = jnp.concatenate([w_in[:, :kr0], k_rope], axis=1).astype(BF16)
    w_gate = w_in[:, kr0 + QK_ROPE_DIM:].astype(BF16)
    per_head = w_q_up.reshape(Q_LORA_RANK, N_HEADS, Q_HEAD_DIM)
    w_q_nope = per_head[:, :, :QK_NOPE_DIM].reshape(Q_LORA_RANK, N_HEADS * QK_NOPE_DIM)
    rope = per_head[:, :, QK_NOPE_DIM:].reshape(Q_LORA_RANK, N_PAIRS, 2, 2, ROPE_HALF)
    w_q_rope = rope.transpose(0, 1, 3, 2, 4).reshape(Q_LORA_RANK, N_PAIRS * 2 * QK_ROPE_DIM)
    return w_lat, w_gate, w_q_nope.astype(BF16), w_q_rope.astype(BF16)


def _mla_proj_kernel(x_ref, pos_ref, g_ref, wlat_ref, wgate_ref, qn_ref, wqn_ref, wqr_ref,
                     kvn_ref, wkv_ref, rc_ref, q_ref, k_ref, v_ref, z_ref):
    tile = x_ref.shape[0]
    xn = _rms_scale(x_ref[...], g_ref[...]).astype(BF16)
    kv0 = Q_LORA_RANK
    kr0 = kv0 + KV_LORA_RANK
    lat = _dot(xn, wlat_ref[...])
    q_lat, kv_lat, kr = lat[:, :kv0], lat[:, kv0:kr0], lat[:, kr0:]

    pos_row = pos_ref[...].astype(F32)
    pos_col = jnp.broadcast_to(pos_row, (V7X_SUBLANES, tile)).T[:, 0:1]

    rows = tile // PROJ_PIECES
    heads = N_HEADS // PROJ_PIECES
    cos_rows, sin_rows = [], []
    for c in range(PROJ_PIECES):
        z = _dot(xn, wgate_ref[:, c * heads * V_HEAD_DIM:(c + 1) * heads * V_HEAD_DIM])
        for h in range(heads):
            z_ref[c * heads + h] = z[:, h * V_HEAD_DIM:(h + 1) * V_HEAD_DIM]
        angles = pos_col[c * rows:(c + 1) * rows, :] * rc_ref[0:1, :]
        cos_rows.append(jnp.cos(angles))
        sin_rows.append(jnp.sin(angles) * rc_ref[1:2, :])
    cos = jnp.concatenate(cos_rows, axis=0)
    sin = jnp.concatenate(sin_rows, axis=0)

    def rope(t):
        return t * cos + pltpu.roll(t, 2 * ROPE_HALF, axis=1) * sin

    qn = _rms_scale(q_lat, qn_ref[...]).astype(BF16)
    q_nope = _dot(qn, wqn_ref[...]) * EXP2_SCALE
    q_rope = _dot(qn, wqr_ref[...]) * EXP2_SCALE
    for p in range(N_PAIRS):
        pair_rope = rope(q_rope[:, p * V7X_LANES:(p + 1) * V7X_LANES]).astype(BF16)
        for h in (2 * p, 2 * p + 1):
            nope = q_nope[:, h * QK_NOPE_DIM:(h + 1) * QK_NOPE_DIM].astype(BF16)
            if h % 2 == 0:
                q_ref[h, :, :QK_NOPE_DIM] = nope
                q_ref[h, :, QK_NOPE_DIM:] = pair_rope
            else:
                q_ref[h, :, :QK_NOPE_DIM] = pair_rope
                q_ref[h, :, QK_NOPE_DIM:] = nope

    kr = rope(kr)
    kr_even = (kr * rc_ref[2:3, :]).astype(BF16)
    kr_odd = (kr * rc_ref[3:4, :]).astype(BF16)
    kvn = _rms_scale(kv_lat, kvn_ref[...]).astype(BF16)
    for c in range(PROJ_PIECES):
        kv = _dot(kvn, wkv_ref[:, c * heads * HEAD_K:(c + 1) * heads * HEAD_K])
        for h in range(c * heads, (c + 1) * heads):
            lo = (h - c * heads) * HEAD_K
            k_nope = kv[:, lo:lo + QK_NOPE_DIM].astype(BF16)
            v_ref[h] = kv[:, lo + QK_NOPE_DIM:lo + HEAD_K].astype(BF16)
            if h % 2 == 0:
                k_ref[h, :, :QK_NOPE_DIM] = k_nope
                k_ref[h, :, QK_NOPE_DIM:] = kr_even
            else:
                k_ref[h, :, :QK_NOPE_DIM] = kr_odd
                k_ref[h, :, QK_NOPE_DIM:] = k_nope


def _mla_proj(x, positions, g, w_in, q_norm, w_q_up, kv_norm, w_kv_up):
    B, S, D = x.shape
    tile = SEQ_TILE
    w_lat, w_gate, w_q_nope, w_q_rope = _mla_proj_weights(w_in, w_q_up)

    def head_spec(width):
        return pl.BlockSpec((None, N_HEADS, tile, width), lambda b, s: (b, 0, s, 0))

    def head_shape(width, dtype):
        return jax.ShapeDtypeStruct((B, N_HEADS, S, width), dtype)

    return pl.pallas_call(
        _mla_proj_kernel,
        grid=(B, S // tile),
        in_specs=[
            _row_spec(tile, D),
            pl.BlockSpec((None, 1, tile), lambda b, s: (b, 0, s)),
            _const_spec((1, D)),
            _const_spec(w_lat.shape),
            _const_spec(w_gate.shape),
            _const_spec((1, Q_LORA_RANK)),
            _const_spec(w_q_nope.shape),
            _const_spec(w_q_rope.shape),
            _const_spec((1, KV_LORA_RANK)),
            _const_spec((KV_LORA_RANK, N_HEADS * HEAD_K)),
            _const_spec((V7X_SUBLANES, V7X_LANES)),
        ],
        out_specs=[head_spec(HEAD_K), head_spec(HEAD_K),
                   head_spec(V_HEAD_DIM), head_spec(V_HEAD_DIM)],
        out_shape=[head_shape(HEAD_K, BF16), head_shape(HEAD_K, BF16),
                   head_shape(V_HEAD_DIM, BF16), head_shape(V_HEAD_DIM, F32)],
        compiler_params=_params(("arbitrary", "arbitrary")),
        name="mla_proj",
    )(x, positions.reshape(B, 1, S), g.reshape(1, D), w_lat, w_gate, q_norm.reshape(1, -1),
      w_q_nope, w_q_rope, kv_norm.reshape(1, -1), w_kv_up.astype(BF16),
      jnp.asarray(_rope_constants()))


def _attn_kernel(q_ref, k_ref, v_ref, z_ref, o_ref, s_even_ref, s_odd_ref, vxt_ref, *, seq):
    nt = (((1,), (1,)), ((), ()))
    n_tiles = seq // ATTN_TQ
    half = ATTN_TQ // 2
    traced_zero = jnp.minimum(pl.program_id(1), 0)
    s_refs = (s_even_ref, s_odd_ref)
    vxt_ref[:V_HEAD_DIM, :] = v_ref[...].astype(F32).T.astype(BF16)
    pad_row = lax.broadcasted_iota(jnp.int32, (V7X_BF16_SUBLANES, seq), 0)
    vxt_ref[V_HEAD_DIM:, :] = jnp.where(pad_row == 0, 1.0, 0.0).astype(BF16)

    def causal(s, first_key, first_query):
        key = first_key + lax.broadcasted_iota(jnp.int32, s.shape, 0)
        query = first_query + lax.broadcasted_iota(jnp.int32, s.shape, 1)
        return jnp.where(key <= query, s, MASK_VALUE)

    def col_max(m, s):
        top = jnp.max(s, axis=0, keepdims=True)
        return top if m is None else jnp.maximum(m, top)

    def scores(i, result):
        s_ref, buf, lo, hi = s_refs[i % 2], traced_zero, i * ATTN_TQ, (i + 1) * ATTN_TQ
        mid = lo + half
        q = q_ref[lo:hi, :]
        m = None
        for c0 in range(0, lo, ATTN_CHUNK):
            c1 = min(c0 + ATTN_CHUNK, lo)
            s = lax.dot_general(k_ref[c0:c1, :], q, nt, preferred_element_type=F32)
            s_ref[buf, c0:c1, :] = s
            m = col_max(m, s)
            yield
        s = causal(lax.dot_general(k_ref[lo:mid, :], q, nt, preferred_element_type=F32), lo, lo)
        s_ref[buf, lo:mid, :] = s
        m = col_max(m, s)
        s = causal(lax.dot_general(k_ref[mid:hi, :], q[half:, :], nt,
                                   preferred_element_type=F32), mid, mid)
        s_ref[buf, mid:hi, half:] = s
        result.append(jnp.concatenate([m[:, :half], col_max(m[:, half:], s)], axis=1))

    def weights(i, mc):
        s_ref, buf, lo, hi = s_refs[i % 2], traced_zero, i * ATTN_TQ, (i + 1) * ATTN_TQ
        mid = lo + half

        def probs(r0, r1, c0, c1):
            return jnp.exp2(s_ref[buf, r0:r1, c0:c1] - mc[:, c0:c1]).astype(BF16)

        acc = None
        for c0 in range(0, mid, ATTN_CHUNK):
            c1 = min(c0 + ATTN_CHUNK, mid)
            part = _dot(vxt_ref[:, c0:c1], probs(c0, c1, 0, ATTN_TQ))
            acc = part if acc is None else acc + part
            yield
        late = _dot(vxt_ref[:, mid:hi], probs(mid, hi, half, ATTN_TQ))
        acc = jnp.concatenate([acc[:, :half], acc[:, half:] + late], axis=1)
        o_t = acc[:V_HEAD_DIM, :] / acc[V_HEAD_DIM:V_HEAD_DIM + 1, :]
        o_ref[lo:hi, :] = (o_t.T * _silu(z_ref[lo:hi, :])).astype(BF16)

    def emit_interleaved(*pieces):
        live = list(pieces)
        while live:
            for piece in list(live):
                if next(piece, "done") == "done":
                    live.remove(piece)

    order = list(range(n_tiles - 1, -1, -1))
    result = []
    emit_interleaved(scores(order[0], result))
    for n, i in enumerate(order):
        mc = result.pop()
        if n + 1 < n_tiles:
            emit_interleaved(scores(order[n + 1], result), weights(i, mc))
        else:
            emit_interleaved(weights(i, mc))


def _attention(q, k, v, z):
    B, _, S, _ = q.shape

    def head_spec(width):
        return pl.BlockSpec((None, None, S, width), lambda b, h: (b, h, 0, 0))

    return pl.pallas_call(
        functools.partial(_attn_kernel, seq=S),
        grid=(B, N_HEADS),
        in_specs=[head_spec(HEAD_K), head_spec(HEAD_K),
                  head_spec(V_HEAD_DIM), head_spec(V_HEAD_DIM)],
        out_specs=pl.BlockSpec((None, S, V_HEAD_DIM), lambda b, h: (b, 0, h)),
        out_shape=jax.ShapeDtypeStruct((B, S, D_INNER), BF16),
        scratch_shapes=[
            pltpu.VMEM((1, S, ATTN_TQ), F32),
            pltpu.VMEM((1, S, ATTN_TQ), F32),
            pltpu.VMEM((V_HEAD_DIM + V7X_BF16_SUBLANES, S), BF16),
        ],
        compiler_params=_params(("arbitrary", "arbitrary")),
        name="mla_attention",
    )(q, k, v, z)


def _out_proj_kernel(x_ref, a_ref, w_ref, o_ref):
    o_ref[...] = x_ref[...] + _dot(a_ref[...], w_ref[...])


def _out_proj(x, a, w_out):
    B, S, D = x.shape
    tile = OUT_PROJ_TILE
    return pl.pallas_call(
        _out_proj_kernel,
        grid=(B, S // tile),
        in_specs=[_row_spec(tile, D), _row_spec(tile, D_INNER), _const_spec((D_INNER, D))],
        out_specs=_row_spec(tile, D),
        out_shape=jax.ShapeDtypeStruct((B, S, D), F32),
        compiler_params=_params(("arbitrary", "arbitrary")),
        name="mla_out_proj",
    )(x, a, w_out)


def _mla_layer(x, positions, g, w_in, q_norm, w_q_up, kv_norm, w_kv_up, w_out):
    q, k, v, z = _mla_proj(x, positions, g, w_in, q_norm, w_q_up, kv_norm, w_kv_up)
    gated = _attention(q, k, v, z)
    return _out_proj(x, gated, w_out.astype(BF16))


def kernel(x, positions, pool_norm, pool_w_in, pool_w_grp, pool_scale, pool_w_out,
           conv_norm, conv_w_in, conv_w, conv_w_out,
           mla_norm, mla_w_in, mla_q_norm, mla_w_q_up, mla_kv_norm, mla_w_kv_up, mla_w_out,
           final_norm):
    assert (DEPTH - 1) % N_MIXERS == 0, "the final norm is fused into a closing pool layer"
    for i in range(DEPTH):
        kind, j = i % N_MIXERS, i // N_MIXERS
        last = i == DEPTH - 1
        if kind == 0:
            x = _pool_layer(x, j, pool_norm, pool_w_in, pool_w_grp, pool_scale, pool_w_out,
                            final_norm if last else None)
        elif kind == 1:
            x = _conv_layer(x, j, conv_norm, conv_w_in, conv_w, conv_w_out)
        else:
            x = _mla_layer(x, positions, mla_norm[j], mla_w_in[j], mla_q_norm[j],
                           mla_w_q_up[j], mla_kv_norm[j], mla_w_kv_up[j], mla_w_out[j])
    return x
```

```python
import functools
import math

import numpy as np
import jax
import jax.numpy as jnp
from jax import lax
from jax.experimental import pallas as pl
from jax.experimental.pallas import tpu as pltpu

D_MODEL = 1024
DEPTH = 4
N_MIXERS = 3
D_INNER = 2 * D_MODEL
POOL_WINDOWS = (2, 4, 8, 16)
N_POOL_GROUPS = len(POOL_WINDOWS)
POOL_GROUP = D_INNER // N_POOL_GROUPS
CONV_WIDTH = 3
N_HEADS = 16
N_PAIRS = N_HEADS // 2
QK_NOPE_DIM = 128
QK_ROPE_DIM = 64
ROPE_HALF = QK_ROPE_DIM // 2
V_HEAD_DIM = D_INNER // N_HEADS
Q_LORA_RANK = 384
KV_LORA_RANK = 256
Q_HEAD_DIM = QK_NOPE_DIM + QK_ROPE_DIM
ATTN_SCALE = Q_HEAD_DIM ** -0.5
ROPE_BASE = 10000.0
NORM_EPS = 1e-6
MASK_VALUE = -1e30

V7X_LANES = 128
V7X_SUBLANES = 8
V7X_BF16_SUBLANES = 16
V7X_VMEM_LIMIT_BYTES = 56 * 1024 * 1024

POOL_HALO = max(POOL_WINDOWS)
CONV_HALO = V7X_SUBLANES
SEQ_TILE = 512
PROJ_PIECES = 4
POOL_TILE = 1024
CONV_TILE = 1024
POOL_LOAD_STEPS = 8
CONV_LOAD_STEPS = 16
OUT_PROJ_TILE = 1024
CONV_CHUNK = 512
ATTN_HEADS = 2
ATTN_TQ = 512
ATTN_CHUNK = 512
EXP2_SCALE = ATTN_SCALE * math.log2(math.e)
HEAD_K = 2 * QK_NOPE_DIM

F32 = jnp.float32
BF16 = jnp.bfloat16


def _rms_scale(x, g):
    ms = jnp.mean(x * x, axis=-1, keepdims=True)
    return x * lax.rsqrt(ms + NORM_EPS) * g


def _silu(z):
    return z * (1.0 / (1.0 + jnp.exp(-z)))


def _dot(a, b):
    return jnp.dot(a, b, preferred_element_type=F32)


def _const_spec(shape):
    zeros = (0,) * len(shape)
    return pl.BlockSpec(shape, lambda *_: zeros, pipeline_mode=pl.Buffered(1))


def _layer_spec(shape, j):
    index = (j,) + (0,) * len(shape)
    return pl.BlockSpec((None,) + tuple(shape), lambda *_: index, pipeline_mode=pl.Buffered(1))


def _row_spec(tile, width):
    return pl.BlockSpec((None, tile, width), lambda b, s: (b, s, 0))


def _params(semantics):
    return pltpu.CompilerParams(dimension_semantics=semantics,
                                vmem_limit_bytes=V7X_VMEM_LIMIT_BYTES)


def _phase_row_spec(tile, width, n_seq, n_load):
    def index(t):
        c = jnp.maximum(t - n_load, 0)
        return (c // n_seq, c % n_seq, 0)
    return pl.BlockSpec((None, tile, width), index)


def _weight_chunk_spec(rows, cols, j, n_load):
    return pl.BlockSpec((None, rows // n_load, cols),
                        lambda t: (j, jnp.minimum(t, n_load - 1), 0))


def _store_weight_chunk(dst_ref, src_ref, t):
    chunk = src_ref.shape[0]
    dst_ref[pl.ds(pl.multiple_of(t * chunk, chunk), chunk), :] = src_ref[...].astype(BF16)


def _pool_kernel(*refs, tile, n_seq, final):
    if final:
        (x_ref, g_ref, win_f32, wgrp_f32, scale_ref, wout_f32, fin_ref, o_ref,
         win_ref, wgrp_ref, wout_ref, tail_ref) = refs
    else:
        (x_ref, g_ref, win_f32, wgrp_f32, scale_ref, wout_f32, o_ref,
         win_ref, wgrp_ref, wout_ref, tail_ref) = refs
    t = pl.program_id(0)

    @pl.when(t < POOL_LOAD_STEPS)
    def _():
        _store_weight_chunk(win_ref, win_f32, t)
        _store_weight_chunk(wgrp_ref, wgrp_f32, t)
        _store_weight_chunk(wout_ref, wout_f32, t)

    @pl.when(t >= POOL_LOAD_STEPS)
    def _():
        s_idx = lax.rem(t - POOL_LOAD_STEPS, n_seq)

        @pl.when(s_idx == 0)
        def _():
            tail_ref[...] = jnp.zeros_like(tail_ref)

        x = x_ref[...]
        xn = _rms_scale(x, g_ref[...]).astype(BF16)
        pos = s_idx * tile + lax.broadcasted_iota(jnp.int32, (tile, 1), 0)
        count = (pos + 1).astype(F32)
        acc = None
        for g, w in enumerate(POOL_WINDOWS):
            lo, hi = g * POOL_GROUP, (g + 1) * POOL_GROUP
            u = _dot(xn, win_ref[:, lo:hi])
            z = _dot(xn, win_ref[:, D_INNER + lo:D_INNER + hi])
            s = jnp.concatenate([tail_ref[:, lo:hi], u], axis=0)
            k = 1
            while k < w:
                s = s + pltpu.roll(s, k, axis=0)
                k *= 2
            s = s[POOL_HALO:, :]
            inv = 1.0 / jnp.minimum(count, float(w))
            pooled = (s * inv - u).astype(BF16)
            mixed = _dot(pooled, wgrp_ref[lo:hi, :]) * scale_ref[:, lo:hi]
            gated = (mixed * _silu(z)).astype(BF16)
            part = _dot(gated, wout_ref[lo:hi, :])
            acc = part if acc is None else acc + part
            tail_ref[:, lo:hi] = u[tile - POOL_HALO:, :]
        y = x + acc
        if final:
            y = _rms_scale(y, fin_ref[...])
        o_ref[...] = y


def _pool_layer(x, j, g, w_in, w_grp, scale, w_out, final_g=None):
    B, S, D = x.shape
    tile = POOL_TILE
    n_seq = S // tile
    final = final_g is not None
    n_layers = g.shape[0]
    grp_rows = N_POOL_GROUPS * POOL_GROUP
    in_specs = [
        _phase_row_spec(tile, D, n_seq, POOL_LOAD_STEPS),
        _layer_spec((1, D), j),
        _weight_chunk_spec(D, 2 * D_INNER, j, POOL_LOAD_STEPS),
        _weight_chunk_spec(grp_rows, POOL_GROUP, j, POOL_LOAD_STEPS),
        _layer_spec((1, D_INNER), j),
        _weight_chunk_spec(D_INNER, D, j, POOL_LOAD_STEPS),
    ]
    args = [x, g.reshape(n_layers, 1, D), w_in, w_grp.reshape(n_layers, grp_rows, POOL_GROUP),
            scale.reshape(n_layers, 1, D_INNER), w_out]
    if final:
        in_specs.append(_const_spec((1, D)))
        args.append(final_g.reshape(1, D))
    return pl.pallas_call(
        functools.partial(_pool_kernel, tile=tile, n_seq=n_seq, final=final),
        grid=(POOL_LOAD_STEPS + B * n_seq,),
        in_specs=in_specs,
        out_specs=_phase_row_spec(tile, D, n_seq, POOL_LOAD_STEPS),
        out_shape=jax.ShapeDtypeStruct((B, S, D), F32),
        scratch_shapes=[pltpu.VMEM((D, 2 * D_INNER), BF16),
                        pltpu.VMEM((grp_rows, POOL_GROUP), BF16),
                        pltpu.VMEM((D_INNER, D), BF16),
                        pltpu.VMEM((POOL_HALO, D_INNER), F32)],
        compiler_params=_params(("arbitrary",)),
        name="pool_layer",
    )(*args)


def _conv_kernel(x_ref, g_ref, win_f32, cw_ref, wout_f32, o_ref, win_ref, wout_ref, tail_ref,
                 *, tile, n_seq):
    t = pl.program_id(0)

    @pl.when(t < CONV_LOAD_STEPS)
    def _():
        _store_weight_chunk(win_ref, win_f32, t)
        _store_weight_chunk(wout_ref, wout_f32, t)

    @pl.when(t >= CONV_LOAD_STEPS)
    def _():
        s_idx = lax.rem(t - CONV_LOAD_STEPS, n_seq)

        @pl.when(s_idx == 0)
        def _():
            tail_ref[...] = jnp.zeros_like(tail_ref)

        x = x_ref[...]
        xn = _rms_scale(x, g_ref[...]).astype(BF16)
        acc = None
        for c in range(D_INNER // CONV_CHUNK):
            lo, hi = c * CONV_CHUNK, (c + 1) * CONV_CHUNK
            bb = _dot(xn, win_ref[:, lo:hi])
            cc = _dot(xn, win_ref[:, D_INNER + lo:D_INNER + hi])
            hh = _dot(xn, win_ref[:, 2 * D_INNER + lo:2 * D_INNER + hi])
            zz = _dot(xn, win_ref[:, 3 * D_INNER + lo:3 * D_INNER + hi])
            ch = cc * hh
            che = jnp.concatenate([tail_ref[:, lo:hi], ch], axis=0)
            cw = cw_ref[:, lo:hi]
            conv = (che * cw[2:3, :] + pltpu.roll(che, 1, axis=0) * cw[1:2, :]
                    + pltpu.roll(che, 2, axis=0) * cw[0:1, :])[CONV_HALO:, :]
            y = (bb * conv * _silu(zz)).astype(BF16)
            part = _dot(y, wout_ref[lo:hi, :])
            acc = part if acc is None else acc + part
            tail_ref[:, lo:hi] = ch[tile - CONV_HALO:, :]
        o_ref[...] = x + acc


def _conv_layer(x, j, g, w_in, conv_w, w_out):
    B, S, D = x.shape
    tile = CONV_TILE
    n_seq = S // tile
    n_layers = g.shape[0]
    return pl.pallas_call(
        functools.partial(_conv_kernel, tile=tile, n_seq=n_seq),
        grid=(CONV_LOAD_STEPS + B * n_seq,),
        in_specs=[
            _phase_row_spec(tile, D, n_seq, CONV_LOAD_STEPS),
            _layer_spec((1, D), j),
            _weight_chunk_spec(D, 4 * D_INNER, j, CONV_LOAD_STEPS),
            _layer_spec((CONV_WIDTH, D_INNER), j),
            _weight_chunk_spec(D_INNER, D, j, CONV_LOAD_STEPS),
        ],
        out_specs=_phase_row_spec(tile, D, n_seq, CONV_LOAD_STEPS),
        out_shape=jax.ShapeDtypeStruct((B, S, D), F32),
        scratch_shapes=[pltpu.VMEM((D, 4 * D_INNER), BF16),
                        pltpu.VMEM((D_INNER, D), BF16),
                        pltpu.VMEM((CONV_HALO, D_INNER), F32)],
        compiler_params=_params(("arbitrary",)),
        name="conv_layer",
    )(x, g.reshape(n_layers, 1, D), w_in, conv_w, w_out)


def _rope_constants():
    inv_freq = ROPE_BASE ** (-np.arange(0, QK_ROPE_DIM, 2, dtype=np.float32) / QK_ROPE_DIM)
    lane_group = np.arange(V7X_LANES) // ROPE_HALF
    rows = np.zeros((V7X_SUBLANES, V7X_LANES), np.float32)
    rows[0] = np.tile(inv_freq.astype(np.float32), V7X_LANES // ROPE_HALF)
    rows[1] = np.where(lane_group < 2, -1.0, 1.0)
    rows[2] = (lane_group % 2 == 0)
    rows[3] = (lane_group % 2 == 1)
    return rows


def _mla_proj_weights(w_in, w_q_up):
    d = w_in.shape[0]
    kr0 = Q_LORA_RANK + KV_LORA_RANK
    k_rope = w_in[:, kr0:kr0 + QK_ROPE_DIM].reshape(d, 2, 1, ROPE_HALF)
    k_rope = jnp.broadcast_to(k_rope, (d, 2, 2, ROPE_HALF)).reshape(d, 2 * QK_ROPE_DIM)
    w_lat = jnp.concatenate([w_in[:, :kr0], k_rope], axis=1).astype(BF16)
    w_gate = w_in[:, kr0 + QK_ROPE_DIM:].astype(BF16)
    per_head = w_q_up.reshape(Q_LORA_RANK, N_HEADS, Q_HEAD_DIM)
    w_q_nope = per_head[:, :, :QK_NOPE_DIM].reshape(Q_LORA_RANK, N_HEADS * QK_NOPE_DIM)
    rope = per_head[:, :, QK_NOPE_DIM:].reshape(Q_LORA_RANK, N_PAIRS, 2, 2, ROPE_HALF)
    w_q_rope = rope.transpose(0, 1, 3, 2, 4).reshape(Q_LORA_RANK, N_PAIRS * 2 * QK_ROPE_DIM)
    return w_lat, w_gate, w_q_nope.astype(BF16), w_q_rope.astype(BF16)


def _mla_proj_kernel(x_ref, pos_ref, g_ref, wlat_ref, wgate_ref, qn_ref, wqn_ref, wqr_ref,
                     kvn_ref, wkv_ref, rc_ref, q_ref, k_ref, v_ref, z_ref):
    tile = x_ref.shape[0]
    xn = _rms_scale(x_ref[...], g_ref[...]).astype(BF16)
    kv0 = Q_LORA_RANK
    kr0 = kv0 + KV_LORA_RANK
    lat = _dot(xn, wlat_ref[...])
    q_lat, kv_lat, kr = lat[:, :kv0], lat[:, kv0:kr0], lat[:, kr0:]

    pos_row = pos_ref[...].astype(F32)
    pos_col = jnp.broadcast_to(pos_row, (V7X_SUBLANES, tile)).T[:, 0:1]

    rows = tile // PROJ_PIECES
    heads = N_HEADS // PROJ_PIECES
    cos_rows, sin_rows = [], []
    for c in range(PROJ_PIECES):
        z = _dot(xn, wgate_ref[:, c * heads * V_HEAD_DIM:(c + 1) * heads * V_HEAD_DIM])
        for h in range(heads):
            z_ref[c * heads + h] = z[:, h * V_HEAD_DIM:(h + 1) * V_HEAD_DIM]
        angles = pos_col[c * rows:(c + 1) * rows, :] * rc_ref[0:1, :]
        cos_rows.append(jnp.cos(angles))
        sin_rows.append(jnp.sin(angles) * rc_ref[1:2, :])
    cos = jnp.concatenate(cos_rows, axis=0)
    sin = jnp.concatenate(sin_rows, axis=0)

    def rope(t):
        return t * cos + pltpu.roll(t, 2 * ROPE_HALF, axis=1) * sin

    qn = _rms_scale(q_lat, qn_ref[...]).astype(BF16)
    q_nope = _dot(qn, wqn_ref[...]) * EXP2_SCALE
    q_rope = _dot(qn, wqr_ref[...]) * EXP2_SCALE
    for p in range(N_PAIRS):
        pair_rope = rope(q_rope[:, p * V7X_LANES:(p + 1) * V7X_LANES]).astype(BF16)
        for h in (2 * p, 2 * p + 1):
            nope = q_nope[:, h * QK_NOPE_DIM:(h + 1) * QK_NOPE_DIM].astype(BF16)
            if h % 2 == 0:
                q_ref[h, :, :QK_NOPE_DIM] = nope
                q_ref[h, :, QK_NOPE_DIM:] = pair_rope
            else:
                q_ref[h, :, :QK_NOPE_DIM] = pair_rope
                q_ref[h, :, QK_NOPE_DIM:] = nope

    kr = rope(kr)
    kr_even = (kr * rc_ref[2:3, :]).astype(BF16)
    kr_odd = (kr * rc_ref[3:4, :]).astype(BF16)
    kvn = _rms_scale(kv_lat, kvn_ref[...]).astype(BF16)
    for c in range(PROJ_PIECES):
        kv = _dot(kvn, wkv_ref[:, c * heads * HEAD_K:(c + 1) * heads * HEAD_K])
        for h in range(c * heads, (c + 1) * heads):
            lo = (h - c * heads) * HEAD_K
            k_nope = kv[:, lo:lo + QK_NOPE_DIM].astype(BF16)
            v_ref[h] = kv[:, lo + QK_NOPE_DIM:lo + HEAD_K].astype(BF16)
            if h % 2 == 0:
                k_ref[h, :, :QK_NOPE_DIM] = k_nope
                k_ref[h, :, QK_NOPE_DIM:] = kr_even
            else:
                k_ref[h, :, :QK_NOPE_DIM] = kr_odd
                k_ref[h, :, QK_NOPE_DIM:] = k_nope


def _mla_proj(x, positions, g, w_in, q_norm, w_q_up, kv_norm, w_kv_up):
    B, S, D = x.shape
    tile = SEQ_TILE
    w_lat, w_gate, w_q_nope, w_q_rope = _mla_proj_weights(w_in, w_q_up)

    def head_spec(width):
        return pl.BlockSpec((None, N_HEADS, tile, width), lambda b, s: (b, 0, s, 0))

    def head_shape(width, dtype):
        return jax.ShapeDtypeStruct((B, N_HEADS, S, width), dtype)

    return pl.pallas_call(
        _mla_proj_kernel,
        grid=(B, S // tile),
        in_specs=[
            _row_spec(tile, D),
            pl.BlockSpec((None, 1, tile), lambda b, s: (b, 0, s)),
            _const_spec((1, D)),
            _const_spec(w_lat.shape),
            _const_spec(w_gate.shape),
            _const_spec((1, Q_LORA_RANK)),
            _const_spec(w_q_nope.shape),
            _const_spec(w_q_rope.shape),
            _const_spec((1, KV_LORA_RANK)),
            _const_spec((KV_LORA_RANK, N_HEADS * HEAD_K)),
            _const_spec((V7X_SUBLANES, V7X_LANES)),
        ],
        out_specs=[head_spec(HEAD_K), head_spec(HEAD_K),
                   head_spec(V_HEAD_DIM), head_spec(V_HEAD_DIM)],
        out_shape=[head_shape(HEAD_K, BF16), head_shape(HEAD_K, BF16),
                   head_shape(V_HEAD_DIM, BF16), head_shape(V_HEAD_DIM, F32)],
        compiler_params=_params(("arbitrary", "arbitrary")),
        name="mla_proj",
    )(x, positions.reshape(B, 1, S), g.reshape(1, D), w_lat, w_gate, q_norm.reshape(1, -1),
      w_q_nope, w_q_rope, kv_norm.reshape(1, -1), w_kv_up.astype(BF16),
      jnp.asarray(_rope_constants()))


def _attn_kernel(q_ref, k_ref, v_ref, z_ref, o_ref, s_even_ref, s_odd_ref, *vxt_refs, seq):
    nt = (((1,), (1,)), ((), ()))
    n_tiles = seq // ATTN_TQ
    half = ATTN_TQ // 2
    traced_zero = jnp.minimum(pl.program_id(1), 0)
    s_refs = (s_even_ref, s_odd_ref)
    pad_row = lax.broadcasted_iota(jnp.int32, (V7X_BF16_SUBLANES, seq), 0)
    for head, vxt_ref in enumerate(vxt_refs):
        vxt_ref[:V_HEAD_DIM, :] = v_ref[head].astype(F32).T.astype(BF16)
        vxt_ref[V_HEAD_DIM:, :] = jnp.where(pad_row == 0, 1.0, 0.0).astype(BF16)

    def causal(s, first_key, first_query):
        key = first_key + lax.broadcasted_iota(jnp.int32, s.shape, 0)
        query = first_query + lax.broadcasted_iota(jnp.int32, s.shape, 1)
        return jnp.where(key <= query, s, MASK_VALUE)

    def col_max(m, s):
        top = jnp.max(s, axis=0, keepdims=True)
        return top if m is None else jnp.maximum(m, top)

    def scores(unit, head, i, result):
        s_ref, buf, lo, hi = s_refs[unit % 2], traced_zero, i * ATTN_TQ, (i + 1) * ATTN_TQ
        mid = lo + half
        q = q_ref[head, lo:hi, :]
        m = None
        for c0 in range(0, lo, ATTN_CHUNK):
            c1 = min(c0 + ATTN_CHUNK, lo)
            s = lax.dot_general(k_ref[head, c0:c1, :], q, nt, preferred_element_type=F32)
            s_ref[buf, c0:c1, :] = s
            m = col_max(m, s)
            yield
        s = causal(lax.dot_general(k_ref[head, lo:mid, :], q, nt,
                                   preferred_element_type=F32), lo, lo)
        s_ref[buf, lo:mid, :] = s
        m = col_max(m, s)
        s = causal(lax.dot_general(k_ref[head, mid:hi, :], q[half:, :], nt,
                                   preferred_element_type=F32), mid, mid)
        s_ref[buf, mid:hi, half:] = s
        result.append(jnp.concatenate([m[:, :half], col_max(m[:, half:], s)], axis=1))

    def weights(unit, head, i, mc):
        s_ref, buf, lo, hi = s_refs[unit % 2], traced_zero, i * ATTN_TQ, (i + 1) * ATTN_TQ
        mid = lo + half
        vxt_ref = vxt_refs[head]

        def probs(r0, r1, c0, c1):
            return jnp.exp2(s_ref[buf, r0:r1, c0:c1] - mc[:, c0:c1]).astype(BF16)

        acc = None
        for c0 in range(0, mid, ATTN_CHUNK):
            c1 = min(c0 + ATTN_CHUNK, mid)
            part = _dot(vxt_ref[:, c0:c1], probs(c0, c1, 0, ATTN_TQ))
            acc = part if acc is None else acc + part
            yield
        late = _dot(vxt_ref[:, mid:hi], probs(mid, hi, half, ATTN_TQ))
        acc = jnp.concatenate([acc[:, :half], acc[:, half:] + late], axis=1)
        o_t = acc[:V_HEAD_DIM, :] / acc[V_HEAD_DIM:V_HEAD_DIM + 1, :]
        o_ref[lo:hi, head * V_HEAD_DIM:(head + 1) * V_HEAD_DIM] = (
            o_t.T * _silu(z_ref[head, lo:hi, :])).astype(BF16)

    def emit_interleaved(*pieces):
        live = list(pieces)
        while live:
            for piece in list(live):
                if next(piece, "done") == "done":
                    live.remove(piece)

    units = [(head, i) for head in range(len(vxt_refs)) for i in range(n_tiles - 1, -1, -1)]
    result = []
    emit_interleaved(scores(0, *units[0], result))
    for n, (head, i) in enumerate(units):
        mc = result.pop()
        if n + 1 < len(units):
            emit_interleaved(scores(n + 1, *units[n + 1], result), weights(n, head, i, mc))
        else:
            emit_interleaved(weights(n, head, i, mc))


def _attention(q, k, v, z):
    B, _, S, _ = q.shape

    def heads_spec(width):
        return pl.BlockSpec((None, ATTN_HEADS, S, width), lambda b, h: (b, h, 0, 0))

    return pl.pallas_call(
        functools.partial(_attn_kernel, seq=S),
        grid=(B, N_HEADS // ATTN_HEADS),
        in_specs=[heads_spec(HEAD_K), heads_spec(HEAD_K),
                  heads_spec(V_HEAD_DIM), heads_spec(V_HEAD_DIM)],
        out_specs=pl.BlockSpec((None, S, ATTN_HEADS * V_HEAD_DIM), lambda b, h: (b, 0, h)),
        out_shape=jax.ShapeDtypeStruct((B, S, D_INNER), BF16),
        scratch_shapes=[
            pltpu.VMEM((1, S, ATTN_TQ), F32),
            pltpu.VMEM((1, S, ATTN_TQ), F32),
        ] + [pltpu.VMEM((V_HEAD_DIM + V7X_BF16_SUBLANES, S), BF16)] * ATTN_HEADS,
        compiler_params=_params(("arbitrary", "arbitrary")),
        name="mla_attention",
    )(q, k, v, z)


def _out_proj_kernel(x_ref, a_ref, w_ref, o_ref):
    o_ref[...] = x_ref[...] + _dot(a_ref[...], w_ref[...])


def _out_proj(x, a, w_out):
    B, S, D = x.shape
    tile = OUT_PROJ_TILE
    return pl.pallas_call(
        _out_proj_kernel,
        grid=(B, S // tile),
        in_specs=[_row_spec(tile, D), _row_spec(tile, D_INNER), _const_spec((D_INNER, D))],
        out_specs=_row_spec(tile, D),
        out_shape=jax.ShapeDtypeStruct((B, S, D), F32),
        compiler_params=_params(("arbitrary", "arbitrary")),
        name="mla_out_proj",
    )(x, a, w_out)


def _mla_layer(x, positions, g, w_in, q_norm, w_q_up, kv_norm, w_kv_up, w_out):
    q, k, v, z = _mla_proj(x, positions, g, w_in, q_norm, w_q_up, kv_norm, w_kv_up)
    gated = _attention(q, k, v, z)
    return _out_proj(x, gated, w_out.astype(BF16))


def kernel(x, positions, pool_norm, pool_w_in, pool_w_grp, pool_scale, pool_w_out,
           conv_norm, conv_w_in, conv_w, conv_w_out,
           mla_norm, mla_w_in, mla_q_norm, mla_w_q_up, mla_kv_norm, mla_w_kv_up, mla_w_out,
           final_norm):
    assert (DEPTH - 1) % N_MIXERS == 0, "the final norm is fused into a closing pool layer"
    for i in range(DEPTH):
        kind, j = i % N_MIXERS, i // N_MIXERS
        last = i == DEPTH - 1
        if kind == 0:
            x = _pool_layer(x, j, pool_norm, pool_w_in, pool_w_grp, pool_scale, pool_w_out,
                            final_norm if last else None)
        elif kind == 1:
            x = _conv_layer(x, j, conv_norm, conv_w_in, conv_w, conv_w_out)
        else:
            x = _mla_layer(x, positions, mla_norm[j], mla_w_in[j], mla_q_norm[j],
                           mla_w_q_up[j], mla_kv_norm[j], mla_w_kv_up[j], mla_w_out[j])
    return x
```

```python
import functools
import math

import numpy as np
import jax
import jax.numpy as jnp
from jax import lax
from jax.experimental import pallas as pl
from jax.experimental.pallas import tpu as pltpu

D_MODEL = 1024
DEPTH = 4
N_MIXERS = 3
D_INNER = 2 * D_MODEL
POOL_WINDOWS = (2, 4, 8, 16)
N_POOL_GROUPS = len(POOL_WINDOWS)
POOL_GROUP = D_INNER // N_POOL_GROUPS
CONV_WIDTH = 3
N_HEADS = 16
N_PAIRS = N_HEADS // 2
QK_NOPE_DIM = 128
QK_ROPE_DIM = 64
ROPE_HALF = QK_ROPE_DIM // 2
V_HEAD_DIM = D_INNER // N_HEADS
Q_LORA_RANK = 384
KV_LORA_RANK = 256
Q_HEAD_DIM = QK_NOPE_DIM + QK_ROPE_DIM
ATTN_SCALE = Q_HEAD_DIM ** -0.5
ROPE_BASE = 10000.0
NORM_EPS = 1e-6
MASK_VALUE = -1e30

V7X_LANES = 128
V7X_SUBLANES = 8
V7X_BF16_SUBLANES = 16
V7X_VMEM_LIMIT_BYTES = 56 * 1024 * 1024

POOL_HALO = max(POOL_WINDOWS)
CONV_HALO = V7X_SUBLANES
SEQ_TILE = 512
PROJ_PIECES = 4
ROPE_PACK = V7X_LANES // ROPE_HALF
POOL_TILE = 1024
CONV_TILE = 1024
POOL_LOAD_STEPS = 8
CONV_LOAD_STEPS = 16
OUT_PROJ_TILE = 1024
CONV_CHUNK = 512
ATTN_HEADS = 2
ATTN_TQ = 512
ATTN_CHUNK = 512
EXP2_SCALE = ATTN_SCALE * math.log2(math.e)
HEAD_K = 2 * QK_NOPE_DIM

F32 = jnp.float32
BF16 = jnp.bfloat16


def _rms_scale(x, g):
    ms = jnp.mean(x * x, axis=-1, keepdims=True)
    return x * lax.rsqrt(ms + NORM_EPS) * g


def _silu(z):
    return z * (1.0 / (1.0 + jnp.exp(-z)))


def _dot(a, b):
    return jnp.dot(a, b, preferred_element_type=F32)


def _const_spec(shape):
    zeros = (0,) * len(shape)
    return pl.BlockSpec(shape, lambda *_: zeros, pipeline_mode=pl.Buffered(1))


def _layer_spec(shape, j):
    index = (j,) + (0,) * len(shape)
    return pl.BlockSpec((None,) + tuple(shape), lambda *_: index, pipeline_mode=pl.Buffered(1))


def _row_spec(tile, width):
    return pl.BlockSpec((None, tile, width), lambda b, s: (b, s, 0))


def _params(semantics):
    return pltpu.CompilerParams(dimension_semantics=semantics,
                                vmem_limit_bytes=V7X_VMEM_LIMIT_BYTES)


def _phase_row_spec(tile, width, n_seq, n_load):
    def index(t):
        c = jnp.maximum(t - n_load, 0)
        return (c // n_seq, c % n_seq, 0)
    return pl.BlockSpec((None, tile, width), index)


def _weight_chunk_spec(rows, cols, j, n_load):
    return pl.BlockSpec((None, rows // n_load, cols),
                        lambda t: (j, jnp.minimum(t, n_load - 1), 0))


def _store_weight_chunk(dst_ref, src_ref, t):
    chunk = src_ref.shape[0]
    dst_ref[pl.ds(pl.multiple_of(t * chunk, chunk), chunk), :] = src_ref[...].astype(BF16)


def _pool_kernel(*refs, tile, n_seq, final):
    if final:
        (x_ref, g_ref, win_f32, wgrp_f32, scale_ref, wout_f32, fin_ref, o_ref,
         win_ref, wgrp_ref, wout_ref, tail_ref) = refs
    else:
        (x_ref, g_ref, win_f32, wgrp_f32, scale_ref, wout_f32, o_ref,
         win_ref, wgrp_ref, wout_ref, tail_ref) = refs
    t = pl.program_id(0)

    @pl.when(t < POOL_LOAD_STEPS)
    def _():
        _store_weight_chunk(win_ref, win_f32, t)
        _store_weight_chunk(wgrp_ref, wgrp_f32, t)
        _store_weight_chunk(wout_ref, wout_f32, t)

    @pl.when(t >= POOL_LOAD_STEPS)
    def _():
        s_idx = lax.rem(t - POOL_LOAD_STEPS, n_seq)

        @pl.when(s_idx == 0)
        def _():
            tail_ref[...] = jnp.zeros_like(tail_ref)

        x = x_ref[...]
        xn = _rms_scale(x, g_ref[...]).astype(BF16)
        pos = s_idx * tile + lax.broadcasted_iota(jnp.int32, (tile, 1), 0)
        count = (pos + 1).astype(F32)
        acc = None
        for g, w in enumerate(POOL_WINDOWS):
            lo, hi = g * POOL_GROUP, (g + 1) * POOL_GROUP
            u = _dot(xn, win_ref[:, lo:hi])
            z = _dot(xn, win_ref[:, D_INNER + lo:D_INNER + hi])
            s = jnp.concatenate([tail_ref[:, lo:hi], u], axis=0)
            k = 1
            while k < w:
                s = s + pltpu.roll(s, k, axis=0)
                k *= 2
            s = s[POOL_HALO:, :]
            inv = 1.0 / jnp.minimum(count, float(w))
            pooled = (s * inv - u).astype(BF16)
            mixed = _dot(pooled, wgrp_ref[lo:hi, :]) * scale_ref[:, lo:hi]
            gated = (mixed * _silu(z)).astype(BF16)
            part = _dot(gated, wout_ref[lo:hi, :])
            acc = part if acc is None else acc + part
            tail_ref[:, lo:hi] = u[tile - POOL_HALO:, :]
        y = x + acc
        if final:
            y = _rms_scale(y, fin_ref[...])
        o_ref[...] = y


def _pool_layer(x, j, g, w_in, w_grp, scale, w_out, final_g=None):
    B, S, D = x.shape
    tile = POOL_TILE
    n_seq = S // tile
    final = final_g is not None
    n_layers = g.shape[0]
    grp_rows = N_POOL_GROUPS * POOL_GROUP
    in_specs = [
        _phase_row_spec(tile, D, n_seq, POOL_LOAD_STEPS),
        _layer_spec((1, D), j),
        _weight_chunk_spec(D, 2 * D_INNER, j, POOL_LOAD_STEPS),
        _weight_chunk_spec(grp_rows, POOL_GROUP, j, POOL_LOAD_STEPS),
        _layer_spec((1, D_INNER), j),
        _weight_chunk_spec(D_INNER, D, j, POOL_LOAD_STEPS),
    ]
    args = [x, g.reshape(n_layers, 1, D), w_in, w_grp.reshape(n_layers, grp_rows, POOL_GROUP),
            scale.reshape(n_layers, 1, D_INNER), w_out]
    if final:
        in_specs.append(_const_spec((1, D)))
        args.append(final_g.reshape(1, D))
    return pl.pallas_call(
        functools.partial(_pool_kernel, tile=tile, n_seq=n_seq, final=final),
        grid=(POOL_LOAD_STEPS + B * n_seq,),
        in_specs=in_specs,
        out_specs=_phase_row_spec(tile, D, n_seq, POOL_LOAD_STEPS),
        out_shape=jax.ShapeDtypeStruct((B, S, D), F32),
        scratch_shapes=[pltpu.VMEM((D, 2 * D_INNER), BF16),
                        pltpu.VMEM((grp_rows, POOL_GROUP), BF16),
                        pltpu.VMEM((D_INNER, D), BF16),
                        pltpu.VMEM((POOL_HALO, D_INNER), F32)],
        compiler_params=_params(("arbitrary",)),
        name="pool_layer",
    )(*args)


def _conv_kernel(x_ref, g_ref, win_f32, cw_ref, wout_f32, o_ref, win_ref, wout_ref, tail_ref,
                 *, tile, n_seq):
    t = pl.program_id(0)

    @pl.when(t < CONV_LOAD_STEPS)
    def _():
        _store_weight_chunk(win_ref, win_f32, t)
        _store_weight_chunk(wout_ref, wout_f32, t)

    @pl.when(t >= CONV_LOAD_STEPS)
    def _():
        s_idx = lax.rem(t - CONV_LOAD_STEPS, n_seq)

        @pl.when(s_idx == 0)
        def _():
            tail_ref[...] = jnp.zeros_like(tail_ref)

        x = x_ref[...]
        xn = _rms_scale(x, g_ref[...]).astype(BF16)
        acc = None
        for c in range(D_INNER // CONV_CHUNK):
            lo, hi = c * CONV_CHUNK, (c + 1) * CONV_CHUNK
            bb = _dot(xn, win_ref[:, lo:hi])
            cc = _dot(xn, win_ref[:, D_INNER + lo:D_INNER + hi])
            hh = _dot(xn, win_ref[:, 2 * D_INNER + lo:2 * D_INNER + hi])
            zz = _dot(xn, win_ref[:, 3 * D_INNER + lo:3 * D_INNER + hi])
            ch = cc * hh
            che = jnp.concatenate([tail_ref[:, lo:hi], ch], axis=0)
            cw = cw_ref[:, lo:hi]
            conv = (che * cw[2:3, :] + pltpu.roll(che, 1, axis=0) * cw[1:2, :]
                    + pltpu.roll(che, 2, axis=0) * cw[0:1, :])[CONV_HALO:, :]
            y = (bb * conv * _silu(zz)).astype(BF16)
            part = _dot(y, wout_ref[lo:hi, :])
            acc = part if acc is None else acc + part
            tail_ref[:, lo:hi] = ch[tile - CONV_HALO:, :]
        o_ref[...] = x + acc


def _conv_layer(x, j, g, w_in, conv_w, w_out):
    B, S, D = x.shape
    tile = CONV_TILE
    n_seq = S // tile
    n_layers = g.shape[0]
    return pl.pallas_call(
        functools.partial(_conv_kernel, tile=tile, n_seq=n_seq),
        grid=(CONV_LOAD_STEPS + B * n_seq,),
        in_specs=[
            _phase_row_spec(tile, D, n_seq, CONV_LOAD_STEPS),
            _layer_spec((1, D), j),
            _weight_chunk_spec(D, 4 * D_INNER, j, CONV_LOAD_STEPS),
            _layer_spec((CONV_WIDTH, D_INNER), j),
            _weight_chunk_spec(D_INNER, D, j, CONV_LOAD_STEPS),
        ],
        out_specs=_phase_row_spec(tile, D, n_seq, CONV_LOAD_STEPS),
        out_shape=jax.ShapeDtypeStruct((B, S, D), F32),
        scratch_shapes=[pltpu.VMEM((D, 4 * D_INNER), BF16),
                        pltpu.VMEM((D_INNER, D), BF16),
                        pltpu.VMEM((CONV_HALO, D_INNER), F32)],
        compiler_params=_params(("arbitrary",)),
        name="conv_layer",
    )(x, g.reshape(n_layers, 1, D), w_in, conv_w, w_out)


def _rope_constants():
    inv_freq = ROPE_BASE ** (-np.arange(0, QK_ROPE_DIM, 2, dtype=np.float32) / QK_ROPE_DIM)
    lane_group = np.arange(V7X_LANES) // ROPE_HALF
    rows = np.zeros((V7X_SUBLANES, V7X_LANES), np.float32)
    rows[0] = np.tile(inv_freq.astype(np.float32), V7X_LANES // ROPE_HALF)
    rows[1] = np.where(lane_group < 2, -1.0, 1.0)
    rows[2] = (lane_group % 2 == 0)
    rows[3] = (lane_group % 2 == 1)
    return rows


def _mla_proj_weights(w_in, w_q_up):
    d = w_in.shape[0]
    kr0 = Q_LORA_RANK + KV_LORA_RANK
    k_rope = w_in[:, kr0:kr0 + QK_ROPE_DIM].reshape(d, 2, 1, ROPE_HALF)
    k_rope = jnp.broadcast_to(k_rope, (d, 2, 2, ROPE_HALF)).reshape(d, 2 * QK_ROPE_DIM)
    w_lat = jnp.concatenate([w_in[:, :kr0], k_rope], axis=1).astype(BF16)
    w_gate = w_in[:, kr0 + QK_ROPE_DIM:].astype(BF16)
    per_head = w_q_up.reshape(Q_LORA_RANK, N_HEADS, Q_HEAD_DIM)
    w_q_nope = per_head[:, :, :QK_NOPE_DIM].reshape(Q_LORA_RANK, N_HEADS * QK_NOPE_DIM)
    rope = per_head[:, :, QK_NOPE_DIM:].reshape(Q_LORA_RANK, N_PAIRS, 2, 2, ROPE_HALF)
    w_q_rope = rope.transpose(0, 1, 3, 2, 4).reshape(Q_LORA_RANK, N_PAIRS * 2 * QK_ROPE_DIM)
    return w_lat, w_gate, w_q_nope.astype(BF16), w_q_rope.astype(BF16)


def _mla_proj_kernel(x_ref, pos_ref, g_ref, wlat_ref, wgate_ref, qn_ref, wqn_ref, wqr_ref,
                     kvn_ref, wkv_ref, rc_ref, q_ref, k_ref, v_ref, z_ref, cos_ref, sin_ref):
    tile = x_ref.shape[0]
    xn = _rms_scale(x_ref[...], g_ref[...]).astype(BF16)
    kv0 = Q_LORA_RANK
    kr0 = kv0 + KV_LORA_RANK
    lat = _dot(xn, wlat_ref[...])
    q_lat, kv_lat, kr = lat[:, :kv0], lat[:, kv0:kr0], lat[:, kr0:]

    heads = N_HEADS // PROJ_PIECES
    for c in range(PROJ_PIECES):
        z = _dot(xn, wgate_ref[:, c * heads * V_HEAD_DIM:(c + 1) * heads * V_HEAD_DIM])
        for h in range(heads):
            z_ref[c * heads + h] = z[:, h * V_HEAD_DIM:(h + 1) * V_HEAD_DIM]

    angles = pos_ref[...].astype(F32) * rc_ref[0:1, :]
    lane_group = lax.broadcasted_iota(jnp.int32, angles.shape, 1) // ROPE_HALF

    def unpack_rows(table, out_ref):
        for g in range(ROPE_PACK):
            own = jnp.where(lane_group == g, table, 0.0)
            own = own + pltpu.roll(own, ROPE_HALF, axis=1)
            own = own + pltpu.roll(own, 2 * ROPE_HALF, axis=1)
            out_ref[pl.ds(g, tile // ROPE_PACK, stride=ROPE_PACK), :] = own
        return out_ref[...]

    cos = unpack_rows(jnp.cos(angles), cos_ref)
    sin = unpack_rows(jnp.sin(angles), sin_ref) * rc_ref[1:2, :]

    def rope(t):
        return t * cos + pltpu.roll(t, 2 * ROPE_HALF, axis=1) * sin

    qn = _rms_scale(q_lat, qn_ref[...]).astype(BF16)
    q_nope = _dot(qn, wqn_ref[...]) * EXP2_SCALE
    q_rope = _dot(qn, wqr_ref[...]) * EXP2_SCALE
    for p in range(N_PAIRS):
        pair_rope = rope(q_rope[:, p * V7X_LANES:(p + 1) * V7X_LANES]).astype(BF16)
        for h in (2 * p, 2 * p + 1):
            nope = q_nope[:, h * QK_NOPE_DIM:(h + 1) * QK_NOPE_DIM].astype(BF16)
            if h % 2 == 0:
                q_ref[h, :, :QK_NOPE_DIM] = nope
                q_ref[h, :, QK_NOPE_DIM:] = pair_rope
            else:
                q_ref[h, :, :QK_NOPE_DIM] = pair_rope
                q_ref[h, :, QK_NOPE_DIM:] = nope

    kr = rope(kr)
    kr_even = (kr * rc_ref[2:3, :]).astype(BF16)
    kr_odd = (kr * rc_ref[3:4, :]).astype(BF16)
    kvn = _rms_scale(kv_lat, kvn_ref[...]).astype(BF16)
    for c in range(PROJ_PIECES):
        kv = _dot(kvn, wkv_ref[:, c * heads * HEAD_K:(c + 1) * heads * HEAD_K])
        for h in range(c * heads, (c + 1) * heads):
            lo = (h - c * heads) * HEAD_K
            k_nope = kv[:, lo:lo + QK_NOPE_DIM].astype(BF16)
            v_ref[h] = kv[:, lo + QK_NOPE_DIM:lo + HEAD_K].astype(BF16)
            if h % 2 == 0:
                k_ref[h, :, :QK_NOPE_DIM] = k_nope
                k_ref[h, :, QK_NOPE_DIM:] = kr_even
            else:
                k_ref[h, :, :QK_NOPE_DIM] = kr_odd
                k_ref[h, :, QK_NOPE_DIM:] = k_nope


def _mla_proj(x, positions, g, w_in, q_norm, w_q_up, kv_norm, w_kv_up):
    B, S, D = x.shape
    tile = SEQ_TILE
    w_lat, w_gate, w_q_nope, w_q_rope = _mla_proj_weights(w_in, w_q_up)
    pos_packed = jnp.repeat(positions.reshape(B, S // ROPE_PACK, ROPE_PACK), ROPE_HALF, axis=2)

    def head_spec(width):
        return pl.BlockSpec((None, N_HEADS, tile, width), lambda b, s: (b, 0, s, 0))

    def head_shape(width, dtype):
        return jax.ShapeDtypeStruct((B, N_HEADS, S, width), dtype)

    return pl.pallas_call(
        _mla_proj_kernel,
        grid=(B, S // tile),
        in_specs=[
            _row_spec(tile, D),
            _row_spec(tile // ROPE_PACK, V7X_LANES),
            _const_spec((1, D)),
            _const_spec(w_lat.shape),
            _const_spec(w_gate.shape),
            _const_spec((1, Q_LORA_RANK)),
            _const_spec(w_q_nope.shape),
            _const_spec(w_q_rope.shape),
            _const_spec((1, KV_LORA_RANK)),
            _const_spec((KV_LORA_RANK, N_HEADS * HEAD_K)),
            _const_spec((V7X_SUBLANES, V7X_LANES)),
        ],
        out_specs=[head_spec(HEAD_K), head_spec(HEAD_K),
                   head_spec(V_HEAD_DIM), head_spec(V_HEAD_DIM)],
        out_shape=[head_shape(HEAD_K, BF16), head_shape(HEAD_K, BF16),
                   head_shape(V_HEAD_DIM, BF16), head_shape(V_HEAD_DIM, F32)],
        scratch_shapes=[pltpu.VMEM((tile, V7X_LANES), F32), pltpu.VMEM((tile, V7X_LANES), F32)],
        compiler_params=_params(("arbitrary", "arbitrary")),
        name="mla_proj",
    )(x, pos_packed, g.reshape(1, D), w_lat, w_gate, q_norm.reshape(1, -1),
      w_q_nope, w_q_rope, kv_norm.reshape(1, -1), w_kv_up.astype(BF16),
      jnp.asarray(_rope_constants()))


def _attn_kernel(q_ref, k_ref, v_ref, z_ref, o_ref, s_even_ref, s_odd_ref, *vxt_refs, seq):
    nt = (((1,), (1,)), ((), ()))
    n_tiles = seq // ATTN_TQ
    half = ATTN_TQ // 2
    traced_zero = jnp.minimum(pl.program_id(1), 0)
    s_refs = (s_even_ref, s_odd_ref)
    pad_row = lax.broadcasted_iota(jnp.int32, (V7X_BF16_SUBLANES, seq), 0)
    for head, vxt_ref in enumerate(vxt_refs):
        vxt_ref[:V_HEAD_DIM, :] = v_ref[head].astype(F32).T.astype(BF16)
        vxt_ref[V_HEAD_DIM:, :] = jnp.where(pad_row == 0, 1.0, 0.0).astype(BF16)

    def causal(s, first_key, first_query):
        key = first_key + lax.broadcasted_iota(jnp.int32, s.shape, 0)
        query = first_query + lax.broadcasted_iota(jnp.int32, s.shape, 1)
        return jnp.where(key <= query, s, MASK_VALUE)

    def col_max(m, s):
        top = jnp.max(s, axis=0, keepdims=True)
        return top if m is None else jnp.maximum(m, top)

    def scores(unit, head, i, result):
        s_ref, buf, lo, hi = s_refs[unit % 2], traced_zero, i * ATTN_TQ, (i + 1) * ATTN_TQ
        mid = lo + half
        q = q_ref[head, lo:hi, :]
        m = None
        for c0 in range(0, lo, ATTN_CHUNK):
            c1 = min(c0 + ATTN_CHUNK, lo)
            s = lax.dot_general(k_ref[head, c0:c1, :], q, nt, preferred_element_type=F32)
            s_ref[buf, c0:c1, :] = s
            m = col_max(m, s)
            yield
        s = causal(lax.dot_general(k_ref[head, lo:mid, :], q, nt,
                                   preferred_element_type=F32), lo, lo)
        s_ref[buf, lo:mid, :] = s
        m = col_max(m, s)
        s = causal(lax.dot_general(k_ref[head, mid:hi, :], q[half:, :], nt,
                                   preferred_element_type=F32), mid, mid)
        s_ref[buf, mid:hi, half:] = s
        result.append(jnp.concatenate([m[:, :half], col_max(m[:, half:], s)], axis=1))

    def weights(unit, head, i, mc):
        s_ref, buf, lo, hi = s_refs[unit % 2], traced_zero, i * ATTN_TQ, (i + 1) * ATTN_TQ
        mid = lo + half
        vxt_ref = vxt_refs[head]

        def probs(r0, r1, c0, c1):
            return jnp.exp2(s_ref[buf, r0:r1, c0:c1] - mc[:, c0:c1]).astype(BF16)

        acc = None
        for c0 in range(0, mid, ATTN_CHUNK):
            c1 = min(c0 + ATTN_CHUNK, mid)
            part = _dot(vxt_ref[:, c0:c1], probs(c0, c1, 0, ATTN_TQ))
            acc = part if acc is None else acc + part
            yield
        late = _dot(vxt_ref[:, mid:hi], probs(mid, hi, half, ATTN_TQ))
        acc = jnp.concatenate([acc[:, :half], acc[:, half:] + late], axis=1)
        o_t = acc[:V_HEAD_DIM, :] / acc[V_HEAD_DIM:V_HEAD_DIM + 1, :]
        o_ref[lo:hi, head * V_HEAD_DIM:(head + 1) * V_HEAD_DIM] = (
            o_t.T * _silu(z_ref[head, lo:hi, :])).astype(BF16)

    def emit_interleaved(*pieces):
        live = list(pieces)
        while live:
            for piece in list(live):
                if next(piece, "done") == "done":
                    live.remove(piece)

    units = [(head, i) for head in range(len(vxt_refs)) for i in range(n_tiles - 1, -1, -1)]
    result = []
    emit_interleaved(scores(0, *units[0], result))
    for n, (head, i) in enumerate(units):
        mc = result.pop()
        if n + 1 < len(units):
            emit_interleaved(scores(n + 1, *units[n + 1], result), weights(n, head, i, mc))
        else:
            emit_interleaved(weights(n, head, i, mc))


def _attention(q, k, v, z):
    B, _, S, _ = q.shape

    def heads_spec(width):
        return pl.BlockSpec((None, ATTN_HEADS, S, width), lambda b, h: (b, h, 0, 0))

    return pl.pallas_call(
        functools.partial(_attn_kernel, seq=S),
        grid=(B, N_HEADS // ATTN_HEADS),
        in_specs=[heads_spec(HEAD_K), heads_spec(HEAD_K),
                  heads_spec(V_HEAD_DIM), heads_spec(V_HEAD_DIM)],
        out_specs=pl.BlockSpec((None, S, ATTN_HEADS * V_HEAD_DIM), lambda b, h: (b, 0, h)),
        out_shape=jax.ShapeDtypeStruct((B, S, D_INNER), BF16),
        scratch_shapes=[
            pltpu.VMEM((1, S, ATTN_TQ), F32),
            pltpu.VMEM((1, S, ATTN_TQ), F32),
        ] + [pltpu.VMEM((V_HEAD_DIM + V7X_BF16_SUBLANES, S), BF16)] * ATTN_HEADS,
        compiler_params=_params(("arbitrary", "arbitrary")),
        name="mla_attention",
    )(q, k, v, z)


def _out_proj_kernel(x_ref, a_ref, w_ref, o_ref):
    o_ref[...] = x_ref[...] + _dot(a_ref[...], w_ref[...])


def _out_proj(x, a, w_out):
    B, S, D = x.shape
    tile = OUT_PROJ_TILE
    return pl.pallas_call(
        _out_proj_kernel,
        grid=(B, S // tile),
        in_specs=[_row_spec(tile, D), _row_spec(tile, D_INNER), _const_spec((D_INNER, D))],
        out_specs=_row_spec(tile, D),
        out_shape=jax.ShapeDtypeStruct((B, S, D), F32),
        compiler_params=_params(("arbitrary", "arbitrary")),
        name="mla_out_proj",
    )(x, a, w_out)


def _mla_layer(x, positions, g, w_in, q_norm, w_q_up, kv_norm, w_kv_up, w_out):
    q, k, v, z = _mla_proj(x, positions, g, w_in, q_norm, w_q_up, kv_norm, w_kv_up)
    gated = _attention(q, k, v, z)
    return _out_proj(x, gated, w_out.astype(BF16))


def kernel(x, positions, pool_norm, pool_w_in, pool_w_grp, pool_scale, pool_w_out,
           conv_norm, conv_w_in, conv_w, conv_w_out,
           mla_norm, mla_w_in, mla_q_norm, mla_w_q_up, mla_kv_norm, mla_w_kv_up, mla_w_out,
           final_norm):
    assert (DEPTH - 1) % N_MIXERS == 0, "the final norm is fused into a closing pool layer"
    for i in range(DEPTH):
        kind, j = i % N_MIXERS, i // N_MIXERS
        last = i == DEPTH - 1
        if kind == 0:
            x = _pool_layer(x, j, pool_norm, pool_w_in, pool_w_grp, pool_scale, pool_w_out,
                            final_norm if last else None)
        elif kind == 1:
            x = _conv_layer(x, j, conv_norm, conv_w_in, conv_w, conv_w_out)
        else:
            x = _mla_layer(x, positions, mla_norm[j], mla_w_in[j], mla_q_norm[j],
                           mla_w_q_up[j], mla_kv_norm[j], mla_w_kv_up[j], mla_w_out[j])
    return x
```

```python
import functools
import math

import numpy as np
import jax
import jax.numpy as jnp
from jax import lax
from jax.experimental import pallas as pl
from jax.experimental.pallas import tpu as pltpu

D_MODEL = 1024
DEPTH = 4
N_MIXERS = 3
D_INNER = 2 * D_MODEL
POOL_WINDOWS = (2, 4, 8, 16)
N_POOL_GROUPS = len(POOL_WINDOWS)
POOL_GROUP = D_INNER // N_POOL_GROUPS
CONV_WIDTH = 3
N_HEADS = 16
N_PAIRS = N_HEADS // 2
QK_NOPE_DIM = 128
QK_ROPE_DIM = 64
ROPE_HALF = QK_ROPE_DIM // 2
V_HEAD_DIM = D_INNER // N_HEADS
Q_LORA_RANK = 384
KV_LORA_RANK = 256
Q_HEAD_DIM = QK_NOPE_DIM + QK_ROPE_DIM
ATTN_SCALE = Q_HEAD_DIM ** -0.5
ROPE_BASE = 10000.0
NORM_EPS = 1e-6
MASK_VALUE = -1e30

V7X_LANES = 128
V7X_SUBLANES = 8
V7X_BF16_SUBLANES = 16
V7X_VMEM_LIMIT_BYTES = 56 * 1024 * 1024

POOL_HALO = max(POOL_WINDOWS)
CONV_HALO = V7X_SUBLANES
SEQ_TILE = 512
PROJ_PIECES = 4
ROPE_PACK = V7X_LANES // ROPE_HALF
POOL_TILE = 1024
CONV_TILE = 1024
POOL_LOAD_STEPS = 8
CONV_LOAD_STEPS = 16
OUT_PROJ_TILE = 1024
CONV_CHUNK = 256
ATTN_HEADS = 2
ATTN_TQ = 512
ATTN_CHUNK = 512
EXP2_SCALE = ATTN_SCALE * math.log2(math.e)
HEAD_K = 2 * QK_NOPE_DIM

F32 = jnp.float32
BF16 = jnp.bfloat16


def _rms_scale(x, g):
    ms = jnp.mean(x * x, axis=-1, keepdims=True)
    return x * lax.rsqrt(ms + NORM_EPS) * g


def _silu(z):
    return z * (1.0 / (1.0 + jnp.exp(-z)))


def _dot(a, b):
    return jnp.dot(a, b, preferred_element_type=F32)


def _const_spec(shape):
    zeros = (0,) * len(shape)
    return pl.BlockSpec(shape, lambda *_: zeros, pipeline_mode=pl.Buffered(1))


def _layer_spec(shape, j):
    index = (j,) + (0,) * len(shape)
    return pl.BlockSpec((None,) + tuple(shape), lambda *_: index, pipeline_mode=pl.Buffered(1))


def _row_spec(tile, width):
    return pl.BlockSpec((None, tile, width), lambda b, s: (b, s, 0))


def _params(semantics):
    return pltpu.CompilerParams(dimension_semantics=semantics,
                                vmem_limit_bytes=V7X_VMEM_LIMIT_BYTES)


def _phase_row_spec(tile, width, n_seq, n_load):
    def index(t):
        c = jnp.maximum(t - n_load, 0)
        return (c // n_seq, c % n_seq, 0)
    return pl.BlockSpec((None, tile, width), index)


def _weight_chunk_spec(rows, cols, j, n_load):
    return pl.BlockSpec((None, rows // n_load, cols),
                        lambda t: (j, jnp.minimum(t, n_load - 1), 0))


def _store_weight_chunk(dst_ref, src_ref, t):
    chunk = src_ref.shape[0]
    dst_ref[pl.ds(pl.multiple_of(t * chunk, chunk), chunk), :] = src_ref[...].astype(BF16)


def _pool_kernel(*refs, tile, n_seq, final):
    if final:
        (x_ref, g_ref, win_f32, wgrp_f32, scale_ref, wout_f32, fin_ref, o_ref,
         win_ref, wgrp_ref, wout_ref, tail_ref) = refs
    else:
        (x_ref, g_ref, win_f32, wgrp_f32, scale_ref, wout_f32, o_ref,
         win_ref, wgrp_ref, wout_ref, tail_ref) = refs
    t = pl.program_id(0)

    @pl.when(t < POOL_LOAD_STEPS)
    def _():
        _store_weight_chunk(win_ref, win_f32, t)
        _store_weight_chunk(wgrp_ref, wgrp_f32, t)
        _store_weight_chunk(wout_ref, wout_f32, t)

    @pl.when(t >= POOL_LOAD_STEPS)
    def _():
        s_idx = lax.rem(t - POOL_LOAD_STEPS, n_seq)

        @pl.when(s_idx == 0)
        def _():
            tail_ref[...] = jnp.zeros_like(tail_ref)

        x = x_ref[...]
        xn = _rms_scale(x, g_ref[...]).astype(BF16)
        pos = s_idx * tile + lax.broadcasted_iota(jnp.int32, (tile, 1), 0)
        count = (pos + 1).astype(F32)
        acc = None
        for g, w in enumerate(POOL_WINDOWS):
            lo, hi = g * POOL_GROUP, (g + 1) * POOL_GROUP
            u = _dot(xn, win_ref[:, lo:hi])
            z = _dot(xn, win_ref[:, D_INNER + lo:D_INNER + hi])
            s = jnp.concatenate([tail_ref[:, lo:hi], u], axis=0)
            k = 1
            while k < w:
                s = s + pltpu.roll(s, k, axis=0)
                k *= 2
            s = s[POOL_HALO:, :]
            inv = 1.0 / jnp.minimum(count, float(w))
            pooled = (s * inv - u).astype(BF16)
            mixed = _dot(pooled, wgrp_ref[lo:hi, :]) * scale_ref[:, lo:hi]
            gated = (mixed * _silu(z)).astype(BF16)
            part = _dot(gated, wout_ref[lo:hi, :])
            acc = part if acc is None else acc + part
            tail_ref[:, lo:hi] = u[tile - POOL_HALO:, :]
        y = x + acc
        if final:
            y = _rms_scale(y, fin_ref[...])
        o_ref[...] = y


def _pool_layer(x, j, g, w_in, w_grp, scale, w_out, final_g=None):
    B, S, D = x.shape
    tile = POOL_TILE
    n_seq = S // tile
    final = final_g is not None
    n_layers = g.shape[0]
    grp_rows = N_POOL_GROUPS * POOL_GROUP
    in_specs = [
        _phase_row_spec(tile, D, n_seq, POOL_LOAD_STEPS),
        _layer_spec((1, D), j),
        _weight_chunk_spec(D, 2 * D_INNER, j, POOL_LOAD_STEPS),
        _weight_chunk_spec(grp_rows, POOL_GROUP, j, POOL_LOAD_STEPS),
        _layer_spec((1, D_INNER), j),
        _weight_chunk_spec(D_INNER, D, j, POOL_LOAD_STEPS),
    ]
    args = [x, g.reshape(n_layers, 1, D), w_in, w_grp.reshape(n_layers, grp_rows, POOL_GROUP),
            scale.reshape(n_layers, 1, D_INNER), w_out]
    if final:
        in_specs.append(_const_spec((1, D)))
        args.append(final_g.reshape(1, D))
    return pl.pallas_call(
        functools.partial(_pool_kernel, tile=tile, n_seq=n_seq, final=final),
        grid=(POOL_LOAD_STEPS + B * n_seq,),
        in_specs=in_specs,
        out_specs=_phase_row_spec(tile, D, n_seq, POOL_LOAD_STEPS),
        out_shape=jax.ShapeDtypeStruct((B, S, D), F32),
        scratch_shapes=[pltpu.VMEM((D, 2 * D_INNER), BF16),
                        pltpu.VMEM((grp_rows, POOL_GROUP), BF16),
                        pltpu.VMEM((D_INNER, D), BF16),
                        pltpu.VMEM((POOL_HALO, D_INNER), F32)],
        compiler_params=_params(("arbitrary",)),
        name="pool_layer",
    )(*args)


def _conv_kernel(x_ref, g_ref, win_f32, cw_ref, wout_f32, o_ref, win_ref, wout_ref, tail_ref,
                 *, tile, n_seq):
    t = pl.program_id(0)

    @pl.when(t < CONV_LOAD_STEPS)
    def _():
        _store_weight_chunk(win_ref, win_f32, t)
        _store_weight_chunk(wout_ref, wout_f32, t)

    @pl.when(t >= CONV_LOAD_STEPS)
    def _():
        s_idx = lax.rem(t - CONV_LOAD_STEPS, n_seq)

        @pl.when(s_idx == 0)
        def _():
            tail_ref[...] = jnp.zeros_like(tail_ref)

        x = x_ref[...]
        xn = _rms_scale(x, g_ref[...]).astype(BF16)
        acc = None
        for c in range(D_INNER // CONV_CHUNK):
            lo, hi = c * CONV_CHUNK, (c + 1) * CONV_CHUNK
            cc = _dot(xn, win_ref[:, D_INNER + lo:D_INNER + hi])
            hh = _dot(xn, win_ref[:, 2 * D_INNER + lo:2 * D_INNER + hi])
            zz = _dot(xn, win_ref[:, 3 * D_INNER + lo:3 * D_INNER + hi])
            bb = _dot(xn, win_ref[:, lo:hi])
            ch = cc * hh
            che = jnp.concatenate([tail_ref[:, lo:hi], ch], axis=0)
            cw = cw_ref[:, lo:hi]
            conv = (che * cw[2:3, :] + pltpu.roll(che, 1, axis=0) * cw[1:2, :]
                    + pltpu.roll(che, 2, axis=0) * cw[0:1, :])[CONV_HALO:, :]
            y = (bb * conv * _silu(zz)).astype(BF16)
            part = _dot(y, wout_ref[lo:hi, :])
            acc = part if acc is None else acc + part
            tail_ref[:, lo:hi] = ch[tile - CONV_HALO:, :]
        o_ref[...] = x + acc


def _conv_layer(x, j, g, w_in, conv_w, w_out):
    B, S, D = x.shape
    tile = CONV_TILE
    n_seq = S // tile
    n_layers = g.shape[0]
    return pl.pallas_call(
        functools.partial(_conv_kernel, tile=tile, n_seq=n_seq),
        grid=(CONV_LOAD_STEPS + B * n_seq,),
        in_specs=[
            _phase_row_spec(tile, D, n_seq, CONV_LOAD_STEPS),
            _layer_spec((1, D), j),
            _weight_chunk_spec(D, 4 * D_INNER, j, CONV_LOAD_STEPS),
            _layer_spec((CONV_WIDTH, D_INNER), j),
            _weight_chunk_spec(D_INNER, D, j, CONV_LOAD_STEPS),
        ],
        out_specs=_phase_row_spec(tile, D, n_seq, CONV_LOAD_STEPS),
        out_shape=jax.ShapeDtypeStruct((B, S, D), F32),
        scratch_shapes=[pltpu.VMEM((D, 4 * D_INNER), BF16),
                        pltpu.VMEM((D_INNER, D), BF16),
                        pltpu.VMEM((CONV_HALO, D_INNER), F32)],
        compiler_params=_params(("arbitrary",)),
        name="conv_layer",
    )(x, g.reshape(n_layers, 1, D), w_in, conv_w, w_out)


def _rope_constants():
    inv_freq = ROPE_BASE ** (-np.arange(0, QK_ROPE_DIM, 2, dtype=np.float32) / QK_ROPE_DIM)
    lane_group = np.arange(V7X_LANES) // ROPE_HALF
    rows = np.zeros((V7X_SUBLANES, V7X_LANES), np.float32)
    rows[0] = np.tile(inv_freq.astype(np.float32), V7X_LANES // ROPE_HALF)
    rows[1] = np.where(lane_group < 2, -1.0, 1.0)
    rows[2] = (lane_group % 2 == 0)
    rows[3] = (lane_group % 2 == 1)
    return rows


def _mla_proj_weights(w_in, w_q_up):
    d = w_in.shape[0]
    kr0 = Q_LORA_RANK + KV_LORA_RANK
    k_rope = w_in[:, kr0:kr0 + QK_ROPE_DIM].reshape(d, 2, 1, ROPE_HALF)
    k_rope = jnp.broadcast_to(k_rope, (d, 2, 2, ROPE_HALF)).reshape(d, 2 * QK_ROPE_DIM)
    w_lat = jnp.concatenate([w_in[:, :kr0], k_rope], axis=1).astype(BF16)
    w_gate = w_in[:, kr0 + QK_ROPE_DIM:].astype(BF16)
    per_head = w_q_up.reshape(Q_LORA_RANK, N_HEADS, Q_HEAD_DIM)
    w_q_nope = per_head[:, :, :QK_NOPE_DIM].reshape(Q_LORA_RANK, N_HEADS * QK_NOPE_DIM)
    rope = per_head[:, :, QK_NOPE_DIM:].reshape(Q_LORA_RANK, N_PAIRS, 2, 2, ROPE_HALF)
    w_q_rope = rope.transpose(0, 1, 3, 2, 4).reshape(Q_LORA_RANK, N_PAIRS * 2 * QK_ROPE_DIM)
    return w_lat, w_gate, w_q_nope.astype(BF16), w_q_rope.astype(BF16)


def _mla_proj_kernel(x_ref, pos_ref, g_ref, wlat_ref, wgate_ref, qn_ref, wqn_ref, wqr_ref,
                     kvn_ref, wkv_ref, rc_ref, q_ref, k_ref, v_ref, z_ref, cos_ref, sin_ref):
    tile = x_ref.shape[0]
    xn = _rms_scale(x_ref[...], g_ref[...]).astype(BF16)
    kv0 = Q_LORA_RANK
    kr0 = kv0 + KV_LORA_RANK
    lat = _dot(xn, wlat_ref[...])
    q_lat, kv_lat, kr = lat[:, :kv0], lat[:, kv0:kr0], lat[:, kr0:]

    heads = N_HEADS // PROJ_PIECES
    for c in range(PROJ_PIECES):
        z = _dot(xn, wgate_ref[:, c * heads * V_HEAD_DIM:(c + 1) * heads * V_HEAD_DIM])
        for h in range(heads):
            z_ref[c * heads + h] = z[:, h * V_HEAD_DIM:(h + 1) * V_HEAD_DIM]

    angles = pos_ref[...].astype(F32) * rc_ref[0:1, :]
    lane_group = lax.broadcasted_iota(jnp.int32, angles.shape, 1) // ROPE_HALF

    def unpack_rows(table, out_ref):
        for g in range(ROPE_PACK):
            own = jnp.where(lane_group == g, table, 0.0)
            own = own + pltpu.roll(own, ROPE_HALF, axis=1)
            own = own + pltpu.roll(own, 2 * ROPE_HALF, axis=1)
            out_ref[pl.ds(g, tile // ROPE_PACK, stride=ROPE_PACK), :] = own
        return out_ref[...]

    cos = unpack_rows(jnp.cos(angles), cos_ref)
    sin = unpack_rows(jnp.sin(angles), sin_ref) * rc_ref[1:2, :]

    def rope(t):
        return t * cos + pltpu.roll(t, 2 * ROPE_HALF, axis=1) * sin

    qn = _rms_scale(q_lat, qn_ref[...]).astype(BF16)
    q_nope = _dot(qn, wqn_ref[...]) * EXP2_SCALE
    q_rope = _dot(qn, wqr_ref[...]) * EXP2_SCALE
    for p in range(N_PAIRS):
        pair_rope = rope(q_rope[:, p * V7X_LANES:(p + 1) * V7X_LANES]).astype(BF16)
        for h in (2 * p, 2 * p + 1):
            nope = q_nope[:, h * QK_NOPE_DIM:(h + 1) * QK_NOPE_DIM].astype(BF16)
            if h % 2 == 0:
                q_ref[h, :, :QK_NOPE_DIM] = nope
                q_ref[h, :, QK_NOPE_DIM:] = pair_rope
            else:
                q_ref[h, :, :QK_NOPE_DIM] = pair_rope
                q_ref[h, :, QK_NOPE_DIM:] = nope

    kr = rope(kr)
    kr_even = (kr * rc_ref[2:3, :]).astype(BF16)
    kr_odd = (kr * rc_ref[3:4, :]).astype(BF16)
    kvn = _rms_scale(kv_lat, kvn_ref[...]).astype(BF16)
    for c in range(PROJ_PIECES):
        kv = _dot(kvn, wkv_ref[:, c * heads * HEAD_K:(c + 1) * heads * HEAD_K])
        for h in range(c * heads, (c + 1) * heads):
            lo = (h - c * heads) * HEAD_K
            k_nope = kv[:, lo:lo + QK_NOPE_DIM].astype(BF16)
            v_ref[h] = kv[:, lo + QK_NOPE_DIM:lo + HEAD_K].astype(BF16)
            if h % 2 == 0:
                k_ref[h, :, :QK_NOPE_DIM] = k_nope
                k_ref[h, :, QK_NOPE_DIM:] = kr_even
            else:
                k_ref[h, :, :QK_NOPE_DIM] = kr_odd
                k_ref[h, :, QK_NOPE_DIM:] = k_nope


def _mla_proj(x, positions, g, w_in, q_norm, w_q_up, kv_norm, w_kv_up):
    B, S, D = x.shape
    tile = SEQ_TILE
    w_lat, w_gate, w_q_nope, w_q_rope = _mla_proj_weights(w_in, w_q_up)
    pos_packed = jnp.repeat(positions.reshape(B, S // ROPE_PACK, ROPE_PACK), ROPE_HALF, axis=2)

    def head_spec(width):
        return pl.BlockSpec((None, N_HEADS, tile, width), lambda b, s: (b, 0, s, 0))

    def head_shape(width, dtype):
        return jax.ShapeDtypeStruct((B, N_HEADS, S, width), dtype)

    return pl.pallas_call(
        _mla_proj_kernel,
        grid=(B, S // tile),
        in_specs=[
            _row_spec(tile, D),
            _row_spec(tile // ROPE_PACK, V7X_LANES),
            _const_spec((1, D)),
            _const_spec(w_lat.shape),
            _const_spec(w_gate.shape),
            _const_spec((1, Q_LORA_RANK)),
            _const_spec(w_q_nope.shape),
            _const_spec(w_q_rope.shape),
            _const_spec((1, KV_LORA_RANK)),
            _const_spec((KV_LORA_RANK, N_HEADS * HEAD_K)),
            _const_spec((V7X_SUBLANES, V7X_LANES)),
        ],
        out_specs=[head_spec(HEAD_K), head_spec(HEAD_K),
                   head_spec(V_HEAD_DIM), head_spec(V_HEAD_DIM)],
        out_shape=[head_shape(HEAD_K, BF16), head_shape(HEAD_K, BF16),
                   head_shape(V_HEAD_DIM, BF16), head_shape(V_HEAD_DIM, F32)],
        scratch_shapes=[pltpu.VMEM((tile, V7X_LANES), F32), pltpu.VMEM((tile, V7X_LANES), F32)],
        compiler_params=_params(("arbitrary", "arbitrary")),
        name="mla_proj",
    )(x, pos_packed, g.reshape(1, D), w_lat, w_gate, q_norm.reshape(1, -1),
      w_q_nope, w_q_rope, kv_norm.reshape(1, -1), w_kv_up.astype(BF16),
      jnp.asarray(_rope_constants()))


def _attn_kernel(q_ref, k_ref, v_ref, z_ref, o_ref, s_even_ref, s_odd_ref, *vxt_refs, seq):
    nt = (((1,), (1,)), ((), ()))
    n_tiles = seq // ATTN_TQ
    half = ATTN_TQ // 2
    traced_zero = jnp.minimum(pl.program_id(1), 0)
    s_refs = (s_even_ref, s_odd_ref)
    pad_row = lax.broadcasted_iota(jnp.int32, (V7X_BF16_SUBLANES, seq), 0)
    for head, vxt_ref in enumerate(vxt_refs):
        vxt_ref[:V_HEAD_DIM, :] = v_ref[head].astype(F32).T.astype(BF16)
        vxt_ref[V_HEAD_DIM:, :] = jnp.where(pad_row == 0, 1.0, 0.0).astype(BF16)

    def causal(s, first_key, first_query):
        key = first_key + lax.broadcasted_iota(jnp.int32, s.shape, 0)
        query = first_query + lax.broadcasted_iota(jnp.int32, s.shape, 1)
        return jnp.where(key <= query, s, MASK_VALUE)

    def col_max(m, s):
        top = jnp.max(s, axis=0, keepdims=True)
        return top if m is None else jnp.maximum(m, top)

    def scores(unit, head, i, result):
        s_ref, buf, lo, hi = s_refs[unit % 2], traced_zero, i * ATTN_TQ, (i + 1) * ATTN_TQ
        mid = lo + half
        q = q_ref[head, lo:hi, :]
        m = None
        for c0 in range(0, lo, ATTN_CHUNK):
            c1 = min(c0 + ATTN_CHUNK, lo)
            s = lax.dot_general(k_ref[head, c0:c1, :], q, nt, preferred_element_type=F32)
            s_ref[buf, c0:c1, :] = s
            m = col_max(m, s)
            yield
        s = causal(lax.dot_general(k_ref[head, lo:mid, :], q, nt,
                                   preferred_element_type=F32), lo, lo)
        s_ref[buf, lo:mid, :] = s
        m = col_max(m, s)
        s = causal(lax.dot_general(k_ref[head, mid:hi, :], q[half:, :], nt,
                                   preferred_element_type=F32), mid, mid)
        s_ref[buf, mid:hi, half:] = s
        result.append(jnp.concatenate([m[:, :half], col_max(m[:, half:], s)], axis=1))

    def weights(unit, head, i, mc):
        s_ref, buf, lo, hi = s_refs[unit % 2], traced_zero, i * ATTN_TQ, (i + 1) * ATTN_TQ
        mid = lo + half
        vxt_ref = vxt_refs[head]

        def probs(r0, r1, c0, c1):
            return jnp.exp2(s_ref[buf, r0:r1, c0:c1] - mc[:, c0:c1]).astype(BF16)

        acc = None
        for c0 in range(0, mid, ATTN_CHUNK):
            c1 = min(c0 + ATTN_CHUNK, mid)
            part = _dot(vxt_ref[:, c0:c1], probs(c0, c1, 0, ATTN_TQ))
            acc = part if acc is None else acc + part
            yield
        late = _dot(vxt_ref[:, mid:hi], probs(mid, hi, half, ATTN_TQ))
        acc = jnp.concatenate([acc[:, :half], acc[:, half:] + late], axis=1)
        o_t = acc[:V_HEAD_DIM, :] / acc[V_HEAD_DIM:V_HEAD_DIM + 1, :]
        o_ref[lo:hi, head * V_HEAD_DIM:(head + 1) * V_HEAD_DIM] = (
            o_t.T * _silu(z_ref[head, lo:hi, :])).astype(BF16)

    def emit_interleaved(*pieces):
        live = list(pieces)
        while live:
            for piece in list(live):
                if next(piece, "done") == "done":
                    live.remove(piece)

    units = [(head, i) for head in range(len(vxt_refs)) for i in range(n_tiles - 1, -1, -1)]
    result = []
    emit_interleaved(scores(0, *units[0], result))
    for n, (head, i) in enumerate(units):
        mc = result.pop()
        if n + 1 < len(units):
            emit_interleaved(scores(n + 1, *units[n + 1], result), weights(n, head, i, mc))
        else:
            emit_interleaved(weights(n, head, i, mc))


def _attention(q, k, v, z):
    B, _, S, _ = q.shape

    def heads_spec(width):
        return pl.BlockSpec((None, ATTN_HEADS, S, width), lambda b, h: (b, h, 0, 0))

    return pl.pallas_call(
        functools.partial(_attn_kernel, seq=S),
        grid=(B, N_HEADS // ATTN_HEADS),
        in_specs=[heads_spec(HEAD_K), heads_spec(HEAD_K),
                  heads_spec(V_HEAD_DIM), heads_spec(V_HEAD_DIM)],
        out_specs=pl.BlockSpec((None, S, ATTN_HEADS * V_HEAD_DIM), lambda b, h: (b, 0, h)),
        out_shape=jax.ShapeDtypeStruct((B, S, D_INNER), BF16),
        scratch_shapes=[
            pltpu.VMEM((1, S, ATTN_TQ), F32),
            pltpu.VMEM((1, S, ATTN_TQ), F32),
        ] + [pltpu.VMEM((V_HEAD_DIM + V7X_BF16_SUBLANES, S), BF16)] * ATTN_HEADS,
        compiler_params=_params(("arbitrary", "arbitrary")),
        name="mla_attention",
    )(q, k, v, z)


def _out_proj_kernel(x_ref, a_ref, w_ref, o_ref):
    o_ref[...] = x_ref[...] + _dot(a_ref[...], w_ref[...])


def _out_proj(x, a, w_out):
    B, S, D = x.shape
    tile = OUT_PROJ_TILE
    return pl.pallas_call(
        _out_proj_kernel,
        grid=(B, S // tile),
        in_specs=[_row_spec(tile, D), _row_spec(tile, D_INNER), _const_spec((D_INNER, D))],
        out_specs=_row_spec(tile, D),
        out_shape=jax.ShapeDtypeStruct((B, S, D), F32),
        compiler_params=_params(("arbitrary", "arbitrary")),
        name="mla_out_proj",
    )(x, a, w_out)


def _mla_layer(x, positions, g, w_in, q_norm, w_q_up, kv_norm, w_kv_up, w_out):
    q, k, v, z = _mla_proj(x, positions, g, w_in, q_norm, w_q_up, kv_norm, w_kv_up)
    gated = _attention(q, k, v, z)
    return _out_proj(x, gated, w_out.astype(BF16))


def kernel(x, positions, pool_norm, pool_w_in, pool_w_grp, pool_scale, pool_w_out,
           conv_norm, conv_w_in, conv_w, conv_w_out,
           mla_norm, mla_w_in, mla_q_norm, mla_w_q_up, mla_kv_norm, mla_w_kv_up, mla_w_out,
           final_norm):
    assert (DEPTH - 1) % N_MIXERS == 0, "the final norm is fused into a closing pool layer"
    for i in range(DEPTH):
        kind, j = i % N_MIXERS, i // N_MIXERS
        last = i == DEPTH - 1
        if kind == 0:
            x = _pool_layer(x, j, pool_norm, pool_w_in, pool_w_grp, pool_scale, pool_w_out,
                            final_norm if last else None)
        elif kind == 1:
            x = _conv_layer(x, j, conv_norm, conv_w_in, conv_w, conv_w_out)
        else:
            x = _mla_layer(x, positions, mla_norm[j], mla_w_in[j], mla_q_norm[j],
                           mla_w_q_up[j], mla_kv_norm[j], mla_w_kv_up[j], mla_w_out[j])
    return x
```
